```python
import jax, jax.numpy as jnp
from jax import lax
import numpy as np

D_MODEL = 2048
BATCH = 2
SEQ = 16384
DEPTH = 1

HEAD_DIM = 128
N_HEADS_FOX = 8
N_HEADS_SB = 8
WIDTH_FOX = N_HEADS_FOX * HEAD_DIM
WIDTH_SB = N_HEADS_SB * HEAD_DIM
D_FF = 4 * D_MODEL
Q_BLOCK = 128
RMS_EPS = 1e-6
NEG_INF = -1e30
IN_COLS = 3 * WIDTH_FOX + N_HEADS_FOX + 3 * WIDTH_SB + 2 * D_MODEL

kernel_name = "hybrid_fox_stickbreaking_gated_block"


def rmsnorm(x, g):
    xf = x.astype(jnp.float32)
    var = jnp.mean(xf * xf, axis=-1, keepdims=True)
    return (xf * lax.rsqrt(var + RMS_EPS) * g.astype(jnp.float32)).astype(x.dtype)


def split_heads(t, n_heads):
    b, s, _ = t.shape
    return t.reshape(b, s, n_heads, HEAD_DIM).transpose(0, 2, 1, 3)


def merge_heads(t):
    b, n, s, d = t.shape
    return t.transpose(0, 2, 1, 3).reshape(b, s, n * d)


def to_query_blocks(t):
    b, h, s = t.shape[:3]
    rest = t.shape[3:]
    t = t.reshape((b, h, s // Q_BLOCK, Q_BLOCK) + rest)
    return jnp.moveaxis(t, 2, 0)


def from_query_blocks(t):
    nb, b, h, q, d = t.shape
    return jnp.moveaxis(t, 0, 2).reshape(b, h, nb * q, d)


def forgetting_attention(q, k, v, log_f):
    seq = q.shape[2]
    scale = HEAD_DIM ** -0.5
    c = lax.cumsum(log_f, axis=2)
    key_pos = jnp.arange(seq)
    starts = jnp.arange(seq // Q_BLOCK) * Q_BLOCK

    def one_block(args):
        q_blk, c_blk, start = args
        q_pos = start + jnp.arange(Q_BLOCK)
        logits = jnp.einsum("bhqd,bhkd->bhqk", q_blk, k) * scale
        logits = logits + c_blk[..., None] - c[:, :, None, :]
        causal = key_pos[None, :] <= q_pos[:, None]
        logits = jnp.where(causal, logits, NEG_INF)
        p = jax.nn.softmax(logits, axis=-1)
        return jnp.einsum("bhqk,bhkd->bhqd", p, v)

    out = lax.map(one_block, (to_query_blocks(q), to_query_blocks(c), starts))
    return from_query_blocks(out)


def stick_breaking_attention(q, k, v):
    seq = q.shape[2]
    scale = HEAD_DIM ** -0.5
    key_pos = jnp.arange(seq)
    starts = jnp.arange(seq // Q_BLOCK) * Q_BLOCK

    def one_block(args):
        q_blk, start = args
        q_pos = start + jnp.arange(Q_BLOCK)
        z = jnp.einsum("bhqd,bhkd->bhqk", q_blk, k) * scale
        strict = key_pos[None, :] < q_pos[:, None]
        log_not_beta = jnp.where(strict, jax.nn.log_sigmoid(-z), 0.0)
        after = lax.cumsum(log_not_beta, axis=3, reverse=True) - log_not_beta
        log_a = jax.nn.log_sigmoid(z) + after
        a = jnp.where(strict, jnp.exp(log_a), 0.0)
        return jnp.einsum("bhqk,bhkd->bhqd", a, v)

    out = lax.map(one_block, (to_query_blocks(q), starts))
    return from_query_blocks(out)


def setup_inputs(seed: int = 0) -> dict:
    key = jax.random.key(seed)
    ks = jax.random.split(key, 12)

    def dense(k, fan_in, fan_out):
        return jax.random.normal(k, (DEPTH, fan_in, fan_out), jnp.float32) * fan_in ** -0.5

    def gain(k, n):
        return 1.0 + 0.02 * jax.random.normal(k, (DEPTH, n), jnp.float32)

    x = jax.random.normal(ks[0], (BATCH, SEQ, D_MODEL), jnp.float32)
    norm_mix_g = gain(ks[1], D_MODEL)
    w_in = dense(ks[2], D_MODEL, IN_COLS)
    b_forget = 2.0 + 0.5 * jax.random.normal(ks[3], (DEPTH, N_HEADS_FOX), jnp.float32)
    w_out_fox = dense(ks[4], WIDTH_FOX, D_MODEL)
    w_out_sb = dense(ks[5], WIDTH_SB, D_MODEL)
    w_out = dense(ks[6], D_MODEL, D_MODEL)
    norm_mlp_g = gain(ks[7], D_MODEL)
    w_mlp_up = dense(ks[8], D_MODEL, D_FF)
    w_mlp_down = dense(ks[9], D_FF, D_MODEL)
    norm_final_g = 1.0 + 0.02 * jax.random.normal(ks[10], (D_MODEL,), jnp.float32)
    return {"x": x, "norm_mix_g": norm_mix_g, "w_in": w_in, "b_forget": b_forget,
            "w_out_fox": w_out_fox, "w_out_sb": w_out_sb, "w_out": w_out,
            "norm_mlp_g": norm_mlp_g, "w_mlp_up": w_mlp_up, "w_mlp_down": w_mlp_down,
            "norm_final_g": norm_final_g}


def reference(x, norm_mix_g, w_in, b_forget, w_out_fox, w_out_sb, w_out,
              norm_mlp_g, w_mlp_up, w_mlp_down, norm_final_g):
    dt = x.dtype
    split_points = np.cumsum([WIDTH_FOX, WIDTH_FOX, WIDTH_FOX, N_HEADS_FOX,
                              WIDTH_SB, WIDTH_SB, WIDTH_SB, D_MODEL])
    for l in range(DEPTH):
        xn = rmsnorm(x, norm_mix_g[l])
        proj = xn @ w_in[l]
        q_a, k_a, v_a, f_a, q_b, k_b, v_b, g_a, g_b = jnp.split(proj, split_points, axis=-1)

        log_f = jax.nn.log_sigmoid((f_a + b_forget[l]).astype(jnp.float32))
        log_f = log_f.transpose(0, 2, 1)
        y_a = forgetting_attention(
            split_heads(q_a, N_HEADS_FOX).astype(jnp.float32),
            split_heads(k_a, N_HEADS_FOX).astype(jnp.float32),
            split_heads(v_a, N_HEADS_FOX).astype(jnp.float32),
            log_f)
        y_a = merge_heads(y_a).astype(dt) @ w_out_fox[l]

        y_b = stick_breaking_attention(
            split_heads(q_b, N_HEADS_SB).astype(jnp.float32),
            split_heads(k_b, N_HEADS_SB).astype(jnp.float32),
            split_heads(v_b, N_HEADS_SB).astype(jnp.float32))
        y_b = merge_heads(y_b).astype(dt) @ w_out_sb[l]

        merged = jax.nn.sigmoid(g_a) * y_a + jax.nn.sigmoid(g_b) * y_b
        x = x + merged @ w_out[l]

        h = rmsnorm(x, norm_mlp_g[l])
        u = jax.nn.relu(h @ w_mlp_up[l])
        x = x + (u * u) @ w_mlp_down[l]

    return rmsnorm(x, norm_final_g)
```

```python
import functools

import jax
import jax.numpy as jnp
from jax import lax
from jax.experimental import pallas as pl
from jax.experimental.pallas import tpu as pltpu

HEAD_DIM = 128
RMS_EPS = 1e-6
MASKED_LOGIT = -1e30
EXP_IS_ZERO_BELOW = -104.0
V7X_LANES = 128
V7X_VMEM_LIMIT_BYTES = 60 * 1024 * 1024

F32 = jnp.float32
BF16 = jnp.bfloat16


def _params(semantics, vmem_bytes=V7X_VMEM_LIMIT_BYTES):
    return pltpu.CompilerParams(dimension_semantics=semantics, vmem_limit_bytes=vmem_bytes)


def _log_sigmoid(u):
    return jnp.minimum(u, 0.0) - jnp.log1p(jnp.exp(-jnp.abs(u)))


def _rms_scale(x, g):
    ms = jnp.mean(x * x, axis=-1, keepdims=True)
    return x * lax.rsqrt(ms + RMS_EPS) * g


def _norm_forget_kernel(x_ref, g_ref, wf_ref, bf_ref, xn_ref, lf_ref):
    xn = _rms_scale(x_ref[...], g_ref[...]).astype(BF16)
    xn_ref[...] = xn
    f = jnp.dot(xn, wf_ref[...], preferred_element_type=F32) + bf_ref[...]
    lf_ref[...] = _log_sigmoid(f)


def _norm_forget(x2d, g, wf, bf, tm):
    t, d = x2d.shape
    return pl.pallas_call(
        _norm_forget_kernel,
        grid=(t // tm,),
        in_specs=[
            pl.BlockSpec((tm, d), lambda i: (i, 0)),
            pl.BlockSpec((1, d), lambda i: (0, 0)),
            pl.BlockSpec((d, V7X_LANES), lambda i: (0, 0)),
            pl.BlockSpec((1, V7X_LANES), lambda i: (0, 0)),
        ],
        out_specs=[
            pl.BlockSpec((tm, d), lambda i: (i, 0)),
            pl.BlockSpec((tm, V7X_LANES), lambda i: (i, 0)),
        ],
        out_shape=[
            jax.ShapeDtypeStruct((t, d), BF16),
            jax.ShapeDtypeStruct((t, V7X_LANES), F32),
        ],
        compiler_params=_params(("parallel",)),
        name="norm_forget",
    )(x2d, g, wf, bf)


def _cumsum_kernel(lf_ref, c_ref, carry_ref, *, n_heads):
    @pl.when(pl.program_id(1) == 0)
    def _():
        carry_ref[...] = jnp.zeros_like(carry_ref)

    tc = lf_ref.shape[1]
    lft = lf_ref[0].T[:n_heads, :]
    row = lax.broadcasted_iota(jnp.int32, (tc, tc), 0)
    col = lax.broadcasted_iota(jnp.int32, (tc, tc), 1)
    upper = (row <= col).astype(F32)
    cs = jnp.dot(lft, upper, precision=lax.Precision.HIGHEST,
                 preferred_element_type=F32) + carry_ref[:, :1]
    c_ref[0] = cs
    carry_ref[...] = jnp.broadcast_to(cs[:, tc - 1:tc], carry_ref.shape)


def _forget_cumsum(lf, n_heads, tc):
    b, s, _ = lf.shape
    return pl.pallas_call(
        functools.partial(_cumsum_kernel, n_heads=n_heads),
        grid=(b, s // tc),
        in_specs=[pl.BlockSpec((1, tc, V7X_LANES), lambda i, j: (i, j, 0))],
        out_specs=pl.BlockSpec((1, n_heads, tc), lambda i, j: (i, 0, j)),
        out_shape=jax.ShapeDtypeStruct((b, n_heads, s), F32),
        scratch_shapes=[pltpu.VMEM((n_heads, V7X_LANES), F32)],
        compiler_params=_params(("arbitrary", "arbitrary")),
        name="forget_cumsum",
    )(lf)


def _proj_heads_kernel(x_ref, w_ref, o_ref):
    acc = jnp.dot(x_ref[...], w_ref[...], preferred_element_type=F32)
    for hh in range(o_ref.shape[2]):
        o_ref[0, 0, hh] = acc[:, hh * HEAD_DIM:(hh + 1) * HEAD_DIM].astype(o_ref.dtype)


def _proj_heads(xn, w, batch, n_heads, tm):
    t, d = xn.shape
    tn = n_heads * HEAD_DIM
    groups = w.shape[1] // tn
    s = t // batch
    nst = s // tm
    return pl.pallas_call(
        _proj_heads_kernel,
        grid=(t // tm, groups),
        in_specs=[
            pl.BlockSpec((tm, d), lambda i, j: (i, 0)),
            pl.BlockSpec((d, tn), lambda i, j: (0, j)),
        ],
        out_specs=pl.BlockSpec((1, 1, n_heads, tm, HEAD_DIM),
                               lambda i, j: (j, i // nst, 0, i % nst, 0)),
        out_shape=jax.ShapeDtypeStruct((groups, batch, n_heads, s, HEAD_DIM), BF16),
        compiler_params=_params(("parallel", "arbitrary")),
        name="proj_heads",
    )(xn, w)


def _proj_gate_kernel(x_ref, w_ref, o_ref):
    acc = jnp.dot(x_ref[...], w_ref[...], preferred_element_type=F32)
    o_ref[...] = jax.nn.sigmoid(acc).astype(o_ref.dtype)


def _proj_gates(xn, w, tm, tn):
    t, d = xn.shape
    n = w.shape[1]
    return pl.pallas_call(
        _proj_gate_kernel,
        grid=(t // tm, n // tn),
        in_specs=[
            pl.BlockSpec((tm, d), lambda i, j: (i, 0)),
            pl.BlockSpec((d, tn), lambda i, j: (0, j)),
        ],
        out_specs=pl.BlockSpec((tm, tn), lambda i, j: (i, j)),
        out_shape=jax.ShapeDtypeStruct((t, n), BF16),
        compiler_params=_params(("parallel", "arbitrary")),
        name="proj_gates",
    )(xn, w)


def _fox_kernel(cstart_ref, cend_ref, q_ref, k_ref, v_ref, c_ref, o_ref, kn_ref, *,
                n_heads, blk, scale):
    b, h, qi = pl.program_id(0), pl.program_id(1), pl.program_id(2)
    bh = b * n_heads + h
    seq = k_ref.shape[3]

    @pl.when(qi == 0)
    def _():
        def chunk_max(i, best):
            kf = k_ref[0, 0, 0, pl.ds(i * blk, blk), :].astype(F32)
            return jnp.maximum(best, jnp.max(jnp.sum(kf * kf, axis=-1)))
        kn_ref[0] = lax.fori_loop(0, seq // blk, chunk_max, jnp.float32(0.0))

    q = q_ref[0, 0, 0]
    qf = q.astype(F32)
    qn2 = jnp.max(jnp.sum(qf * qf, axis=-1))
    qk_bound = jnp.sqrt(qn2 * kn_ref[0]) * scale
    c0 = cstart_ref[bh, qi]

    def block(j, m, l, acc, masked):
        kblk = k_ref[0, 0, 0, pl.ds(j * blk, blk), :]
        vblk = v_ref[0, 0, 0, pl.ds(j * blk, blk), :]
        s = lax.dot_general(q, kblk, (((1,), (1,)), ((), ())),
                            preferred_element_type=F32) * scale
        s = s + (c0 - c_ref[0, j])
        if masked:
            row = lax.broadcasted_iota(jnp.int32, (blk, blk), 0)
            col = lax.broadcasted_iota(jnp.int32, (blk, blk), 1)
            s = jnp.where(col <= row, s, MASKED_LOGIT)
        m_new = jnp.maximum(m, jnp.max(s, axis=-1, keepdims=True))
        p = jnp.exp(s - m_new)
        alpha = jnp.exp(m - m_new)
        l = alpha * l + jnp.sum(p, axis=-1, keepdims=True)
        acc = alpha * acc + jnp.dot(p.astype(BF16), vblk, preferred_element_type=F32)
        return m_new, l, acc

    m0 = jnp.full((blk, 1), MASKED_LOGIT, F32)
    l0 = jnp.zeros((blk, 1), F32)
    a0 = jnp.zeros((blk, HEAD_DIM), F32)
    m, l, acc = block(qi, m0, l0, a0, True)

    threshold = EXP_IS_ZERO_BELOW - 2.0 * qk_bound

    def count_cond(n):
        j = qi - 1 - n
        return jnp.logical_and(j >= 0, c0 - cend_ref[bh, jnp.maximum(j, 0)] >= threshold)

    n_blocks = lax.while_loop(count_cond, lambda n: n + 1, jnp.int32(0))

    def body(n, carry):
        return block(qi - 1 - n, *carry, False)

    m, l, acc = lax.fori_loop(0, n_blocks, body, (m, l, acc))
    o_ref[0] = (acc / l).astype(o_ref.dtype)


def _fox_attention(qkv, c, blk):
    _, batch, n_heads, seq, _ = qkv.shape
    nb = seq // blk
    cflat = c.reshape(batch * n_heads, nb, blk)
    cstart = cflat[:, :, 0]
    cend = cflat[:, :, blk - 1]
    cblocks = cflat.reshape(batch * n_heads, nb, 1, blk)
    kernel = functools.partial(_fox_kernel, n_heads=n_heads, blk=blk, scale=HEAD_DIM ** -0.5)
    grid_spec = pltpu.PrefetchScalarGridSpec(
        num_scalar_prefetch=2,
        grid=(batch, n_heads, nb),
        in_specs=[
            pl.BlockSpec((1, 1, 1, blk, HEAD_DIM), lambda b, h, i, *_: (0, b, h, i, 0)),
            pl.BlockSpec((1, 1, 1, seq, HEAD_DIM), lambda b, h, i, *_: (1, b, h, 0, 0)),
            pl.BlockSpec((1, 1, 1, seq, HEAD_DIM), lambda b, h, i, *_: (2, b, h, 0, 0)),
            pl.BlockSpec((1, nb, 1, blk), lambda b, h, i, *_: (b * n_heads + h, 0, 0, 0)),
        ],
        out_specs=pl.BlockSpec((1, blk, HEAD_DIM), lambda b, h, i, *_: (b, i, h)),
        scratch_shapes=[pltpu.SMEM((1,), F32)],
    )
    return pl.pallas_call(
        kernel,
        grid_spec=grid_spec,
        out_shape=jax.ShapeDtypeStruct((batch, seq, n_heads * HEAD_DIM), BF16),
        compiler_params=_params(("parallel", "parallel", "arbitrary")),
        name="fox_attention",
    )(cstart, cend, qkv, qkv, qkv, cblocks)


def _sb_kernel(q_ref, k_ref, v_ref, o_ref, *, blk, scale):
    qi = pl.program_id(2)
    q = q_ref[0, 0, 0]
    row = lax.broadcasted_iota(jnp.int32, (blk, blk), 0)
    col = lax.broadcasted_iota(jnp.int32, (blk, blk), 1)
    later = (row > col).astype(BF16)
    strict = col < row

    def block(j, carry, acc, masked):
        kblk = k_ref[0, 0, 0, pl.ds(j * blk, blk), :]
        vblk = v_ref[0, 0, 0, pl.ds(j * blk, blk), :]
        z = lax.dot_general(q, kblk, (((1,), (1,)), ((), ())),
                            preferred_element_type=F32) * scale
        log_not_beta = _log_sigmoid(-z)
        if masked:
            log_not_beta = jnp.where(strict, log_not_beta, 0.0)
        hi = log_not_beta.astype(BF16)
        lo = (log_not_beta - hi.astype(F32)).astype(BF16)
        after = (jnp.dot(hi, later, preferred_element_type=F32)
                 + jnp.dot(lo, later, preferred_element_type=F32) + carry)
        a = jnp.exp(z + log_not_beta + after)
        if masked:
            a = jnp.where(strict, a, 0.0)
        acc = acc + jnp.dot(a.astype(BF16), vblk, preferred_element_type=F32)
        carry = carry + jnp.sum(log_not_beta, axis=-1, keepdims=True)
        return carry, acc

    carry0 = jnp.zeros((blk, 1), F32)
    acc0 = jnp.zeros((blk, HEAD_DIM), F32)
    carry, acc = block(qi, carry0, acc0, True)

    def cond(state):
        j, carry, _ = state
        return jnp.logical_and(j >= 0, jnp.max(carry) >= EXP_IS_ZERO_BELOW)

    def body(state):
        j, carry, acc = state
        carry, acc = block(j, carry, acc, False)
        return j - 1, carry, acc

    _, _, acc = lax.while_loop(cond, body, (qi - 1, carry, acc))
    o_ref[0] = acc.astype(o_ref.dtype)


def _sb_attention(qkv, blk):
    _, batch, n_heads, seq, _ = qkv.shape
    nb = seq // blk
    kernel = functools.partial(_sb_kernel, blk=blk, scale=HEAD_DIM ** -0.5)
    return pl.pallas_call(
        kernel,
        grid=(batch, n_heads, nb),
        in_specs=[
            pl.BlockSpec((1, 1, 1, blk, HEAD_DIM), lambda b, h, i: (3, b, h, i, 0)),
            pl.BlockSpec((1, 1, 1, seq, HEAD_DIM), lambda b, h, i: (4, b, h, 0, 0)),
            pl.BlockSpec((1, 1, 1, seq, HEAD_DIM), lambda b, h, i: (5, b, h, 0, 0)),
        ],
        out_specs=pl.BlockSpec((1, blk, HEAD_DIM), lambda b, h, i: (b, i, h)),
        out_shape=jax.ShapeDtypeStruct((batch, seq, n_heads * HEAD_DIM), BF16),
        compiler_params=_params(("parallel", "parallel", "arbitrary")),
        name="sb_attention",
    )(qkv, qkv, qkv)


def _merge_kernel(ya_ref, yb_ref, ga_ref, gb_ref, x_ref, wa_ref, wb_ref, wo_ref, g_ref,
                  x2_ref, h_ref):
    ya = jnp.dot(ya_ref[...], wa_ref[...], preferred_element_type=F32)
    yb = jnp.dot(yb_ref[...], wb_ref[...], preferred_element_type=F32)
    merged = ga_ref[...].astype(F32) * ya + gb_ref[...].astype(F32) * yb
    x2 = x_ref[...] + jnp.dot(merged.astype(BF16), wo_ref[...], preferred_element_type=F32)
    x2_ref[...] = x2
    h_ref[...] = _rms_scale(x2, g_ref[...]).astype(h_ref.dtype)


def _merge_project(ya, yb, gates, x2d, wa, wb, wo, g, tm):
    t, d = x2d.shape
    wa_w = ya.shape[1]
    wb_w = yb.shape[1]
    resident = lambda shape: pl.BlockSpec(shape, lambda i: (0, 0), pipeline_mode=pl.Buffered(1))
    return pl.pallas_call(
        _merge_kernel,
        grid=(t // tm,),
        in_specs=[
            pl.BlockSpec((tm, wa_w), lambda i: (i, 0)),
            pl.BlockSpec((tm, wb_w), lambda i: (i, 0)),
            pl.BlockSpec((tm, d), lambda i: (i, 0)),
            pl.BlockSpec((tm, d), lambda i: (i, 1)),
            pl.BlockSpec((tm, d), lambda i: (i, 0)),
            resident((wa_w, d)),
            resident((wb_w, d)),
            resident((d, d)),
            resident((1, d)),
        ],
        out_specs=[
            pl.BlockSpec((tm, d), lambda i: (i, 0)),
            pl.BlockSpec((tm, d), lambda i: (i, 0)),
        ],
        out_shape=[
            jax.ShapeDtypeStruct((t, d), F32),
            jax.ShapeDtypeStruct((t, d), BF16),
        ],
        compiler_params=_params(("parallel",)),
        name="merge_project",
    )(ya, yb, gates, gates, x2d, wa, wb, wo, g)


def _mlp_kernel(h_ref, wu_ref, wd_ref, x2_ref, g_ref, o_ref, acc_ref, *, final_norm):
    j = pl.program_id(1)

    @pl.when(j == 0)
    def _():
        acc_ref[...] = x2_ref[...]

    u = jnp.maximum(jnp.dot(h_ref[...], wu_ref[...], preferred_element_type=F32), 0.0)
    acc_ref[...] += jnp.dot((u * u).astype(BF16), wd_ref[...], preferred_element_type=F32)

    @pl.when(j == pl.num_programs(1) - 1)
    def _():
        out = acc_ref[...]
        o_ref[...] = (_rms_scale(out, g_ref[...]) if final_norm else out).astype(o_ref.dtype)


def _mlp(h, wu, wd, x2, g, tm, tf, final_norm):
    t, d = x2.shape
    f = wu.shape[1]
    return pl.pallas_call(
        functools.partial(_mlp_kernel, final_norm=final_norm),
        grid=(t // tm, f // tf),
        in_specs=[
            pl.BlockSpec((tm, d), lambda i, j: (i, 0)),
            pl.BlockSpec((d, tf), lambda i, j: (0, j)),
            pl.BlockSpec((tf, d), lambda i, j: (j, 0)),
            pl.BlockSpec((tm, d), lambda i, j: (i, 0)),
            pl.BlockSpec((1, d), lambda i, j: (0, 0)),
        ],
        out_specs=pl.BlockSpec((tm, d), lambda i, j: (i, 0)),
        out_shape=jax.ShapeDtypeStruct((t, d), F32),
        scratch_shapes=[pltpu.VMEM((tm, d), F32)],
        compiler_params=_params(("parallel", "arbitrary")),
        name="mlp",
    )(h, wu, wd, x2, g)


def _tile(n, want):
    t = min(n, want)
    while n % t:
        t //= 2
    return t


def kernel(x, norm_mix_g, w_in, b_forget, w_out_fox, w_out_sb, w_out, norm_mlp_g, w_mlp_up,
           w_mlp_down, norm_final_g):
    batch, seq, d = x.shape
    depth = w_in.shape[0]
    n_heads_fox = b_forget.shape[-1]
    width_fox = w_out_fox.shape[1]
    width_sb = w_out_sb.shape[1]
    n_heads_sb = width_sb // HEAD_DIM
    assert width_fox == n_heads_fox * HEAD_DIM and n_heads_fox == n_heads_sb
    assert n_heads_fox <= 8 and seq % 256 == 0 and d % V7X_LANES == 0
    t = batch * seq
    x2d = x.reshape(t, d)

    for l in range(depth):
        w = w_in[l]
        o_f = 3 * width_fox
        o_sb = o_f + n_heads_fox
        o_g = o_sb + 3 * width_sb
        w_qkv = jnp.concatenate([w[:, :o_f], w[:, o_sb:o_g]], axis=1).astype(BF16)
        w_f = jnp.pad(w[:, o_f:o_sb], ((0, 0), (0, V7X_LANES - n_heads_fox))).astype(BF16)
        b_f = jnp.pad(b_forget[l], (0, V7X_LANES - n_heads_fox)).reshape(1, V7X_LANES)
        w_g = w[:, o_g:].astype(BF16)

        xn, lf = _norm_forget(x2d, norm_mix_g[l].reshape(1, d), w_f, b_f, _tile(t, 512))
        c = _forget_cumsum(lf.reshape(batch, seq, V7X_LANES), n_heads_fox, _tile(seq, 512))
        qkv = _proj_heads(xn, w_qkv, batch, n_heads_fox, _tile(seq, 1024))
        gates = _proj_gates(xn, w_g, _tile(t, 1024), _tile(2 * d, 1024))

        ya = _fox_attention(qkv, c, 256).reshape(t, width_fox)
        yb = _sb_attention(qkv, 256).reshape(t, width_sb)

        x2d, h = _merge_project(ya, yb, gates, x2d, w_out_fox[l].astype(BF16),
                                w_out_sb[l].astype(BF16), w_out[l].astype(BF16),
                                norm_mlp_g[l].reshape(1, d), _tile(t, 512))
        x2d = _mlp(h, w_mlp_up[l].astype(BF16), w_mlp_down[l].astype(BF16), x2d,
                   norm_final_g.reshape(1, d), _tile(t, 512), _tile(w_mlp_up.shape[2], 1024),
                   final_norm=(l == depth - 1))
    return x2d.reshape(batch, seq, d)
```

```python
import functools

import jax
import jax.numpy as jnp
from jax import lax
from jax.experimental import pallas as pl
from jax.experimental.pallas import tpu as pltpu

HEAD_DIM = 128
RMS_EPS = 1e-6
MASKED_LOGIT = -1e30
EXP_IS_ZERO_BELOW = -104.0
LOG2_E = 1.4426950408889634
V7X_LANES = 128
V7X_VMEM_LIMIT_BYTES = 60 * 1024 * 1024

F32 = jnp.float32
BF16 = jnp.bfloat16


def _params(semantics, vmem_bytes=V7X_VMEM_LIMIT_BYTES):
    return pltpu.CompilerParams(dimension_semantics=semantics, vmem_limit_bytes=vmem_bytes)


def _log_sigmoid(u):
    return jnp.minimum(u, 0.0) - jnp.log1p(jnp.exp(-jnp.abs(u)))


def _rms_scale(x, g):
    ms = jnp.mean(x * x, axis=-1, keepdims=True)
    return x * lax.rsqrt(ms + RMS_EPS) * g


def _norm_forget_kernel(x_ref, g_ref, wf_ref, bf_ref, xn_ref, lf_ref):
    xn = _rms_scale(x_ref[...], g_ref[...]).astype(BF16)
    xn_ref[...] = xn
    f = jnp.dot(xn, wf_ref[...], preferred_element_type=F32) + bf_ref[...]
    lf_ref[...] = _log_sigmoid(f)


def _norm_forget(x2d, g, wf, bf, tm):
    t, d = x2d.shape
    return pl.pallas_call(
        _norm_forget_kernel,
        grid=(t // tm,),
        in_specs=[
            pl.BlockSpec((tm, d), lambda i: (i, 0)),
            pl.BlockSpec((1, d), lambda i: (0, 0)),
            pl.BlockSpec((d, V7X_LANES), lambda i: (0, 0)),
            pl.BlockSpec((1, V7X_LANES), lambda i: (0, 0)),
        ],
        out_specs=[
            pl.BlockSpec((tm, d), lambda i: (i, 0)),
            pl.BlockSpec((tm, V7X_LANES), lambda i: (i, 0)),
        ],
        out_shape=[
            jax.ShapeDtypeStruct((t, d), BF16),
            jax.ShapeDtypeStruct((t, V7X_LANES), F32),
        ],
        compiler_params=_params(("parallel",)),
        name="norm_forget",
    )(x2d, g, wf, bf)


def _cumsum_kernel(lf_ref, c_ref, carry_ref, *, n_heads):
    @pl.when(pl.program_id(1) == 0)
    def _():
        carry_ref[...] = jnp.zeros_like(carry_ref)

    tc = lf_ref.shape[1]
    lft = lf_ref[0].T[:n_heads, :]
    row = lax.broadcasted_iota(jnp.int32, (tc, tc), 0)
    col = lax.broadcasted_iota(jnp.int32, (tc, tc), 1)
    upper = (row <= col).astype(F32)
    cs = jnp.dot(lft, upper, precision=lax.Precision.HIGHEST,
                 preferred_element_type=F32) + carry_ref[:, :1]
    c_ref[0] = cs
    carry_ref[...] = jnp.broadcast_to(cs[:, tc - 1:tc], carry_ref.shape)


def _forget_cumsum(lf, n_heads, tc):
    b, s, _ = lf.shape
    return pl.pallas_call(
        functools.partial(_cumsum_kernel, n_heads=n_heads),
        grid=(b, s // tc),
        in_specs=[pl.BlockSpec((1, tc, V7X_LANES), lambda i, j: (i, j, 0))],
        out_specs=pl.BlockSpec((1, n_heads, tc), lambda i, j: (i, 0, j)),
        out_shape=jax.ShapeDtypeStruct((b, n_heads, s), F32),
        scratch_shapes=[pltpu.VMEM((n_heads, V7X_LANES), F32)],
        compiler_params=_params(("arbitrary", "arbitrary")),
        name="forget_cumsum",
    )(lf)


def _proj_heads_kernel(x_ref, w_ref, o_ref):
    acc = jnp.dot(x_ref[...], w_ref[...], preferred_element_type=F32)
    for hh in range(o_ref.shape[2]):
        o_ref[0, 0, hh] = acc[:, hh * HEAD_DIM:(hh + 1) * HEAD_DIM].astype(o_ref.dtype)


def _proj_heads(xn, w, batch, n_heads, tm):
    t, d = xn.shape
    tn = n_heads * HEAD_DIM
    groups = w.shape[1] // tn
    s = t // batch
    nst = s // tm
    return pl.pallas_call(
        _proj_heads_kernel,
        grid=(t // tm, groups),
        in_specs=[
            pl.BlockSpec((tm, d), lambda i, j: (i, 0)),
            pl.BlockSpec((d, tn), lambda i, j: (0, j)),
        ],
        out_specs=pl.BlockSpec((1, 1, n_heads, tm, HEAD_DIM),
                               lambda i, j: (j, i // nst, 0, i % nst, 0)),
        out_shape=jax.ShapeDtypeStruct((groups, batch, n_heads, s, HEAD_DIM), BF16),
        compiler_params=_params(("parallel", "arbitrary")),
        name="proj_heads",
    )(xn, w)


def _proj_gate_kernel(x_ref, w_ref, o_ref):
    acc = jnp.dot(x_ref[...], w_ref[...], preferred_element_type=F32)
    o_ref[...] = jax.nn.sigmoid(acc).astype(o_ref.dtype)


def _proj_gates(xn, w, tm, tn):
    t, d = xn.shape
    n = w.shape[1]
    return pl.pallas_call(
        _proj_gate_kernel,
        grid=(t // tm, n // tn),
        in_specs=[
            pl.BlockSpec((tm, d), lambda i, j: (i, 0)),
            pl.BlockSpec((d, tn), lambda i, j: (0, j)),
        ],
        out_specs=pl.BlockSpec((tm, tn), lambda i, j: (i, j)),
        out_shape=jax.ShapeDtypeStruct((t, n), BF16),
        compiler_params=_params(("parallel", "arbitrary")),
        name="proj_gates",
    )(xn, w)


def _fox_kernel(cstart_ref, cend_ref, q_ref, k_ref, v_ref, c_ref, o_ref, kn_ref, *,
                n_heads, blk, scale):
    b, h, qi = pl.program_id(0), pl.program_id(1), pl.program_id(2)
    bh = b * n_heads + h
    seq = k_ref.shape[3]

    @pl.when(qi == 0)
    def _():
        def chunk_max(i, best):
            kf = k_ref[0, 0, 0, pl.ds(i * blk, blk), :].astype(F32)
            return jnp.maximum(best, jnp.max(jnp.sum(kf * kf, axis=-1)))
        kn_ref[0] = lax.fori_loop(0, seq // blk, chunk_max, jnp.float32(0.0))

    q2 = (q_ref[0, 0, 0].astype(F32) * (scale * LOG2_E)).astype(BF16)
    q2f = q2.astype(F32)
    qn2 = jnp.max(jnp.sum(q2f * q2f, axis=-1))
    qk_bound = jnp.sqrt(qn2 * kn_ref[0]) * (1.001 / LOG2_E)
    c0 = cstart_ref[bh, qi]

    def logits(j, row_bias):
        kblk = k_ref[0, 0, 0, pl.ds(j * blk, blk), :]
        s = lax.dot_general(q2, kblk, (((1,), (1,)), ((), ())), preferred_element_type=F32)
        return s + (row_bias - c_ref[0, j]) * LOG2_E

    def values(j):
        return v_ref[0, 0, 0, pl.ds(j * blk, blk), :]

    j_prev = jnp.maximum(qi - 1, 0)
    s_prev = logits(j_prev, c0 + jnp.where(qi >= 1, 0.0, MASKED_LOGIT))
    row = lax.broadcasted_iota(jnp.int32, (blk, blk), 0)
    col = lax.broadcasted_iota(jnp.int32, (blk, blk), 1)
    s_diag = jnp.where(col <= row, logits(qi, c0), MASKED_LOGIT)
    m = jnp.maximum(jnp.max(s_prev, axis=-1, keepdims=True),
                    jnp.max(s_diag, axis=-1, keepdims=True))
    p_prev = jnp.exp2(s_prev - m)
    p_diag = jnp.exp2(s_diag - m)
    l = jnp.sum(p_prev, axis=-1, keepdims=True) + jnp.sum(p_diag, axis=-1, keepdims=True)
    acc = (jnp.dot(p_prev.astype(BF16), values(j_prev), preferred_element_type=F32)
           + jnp.dot(p_diag.astype(BF16), values(qi), preferred_element_type=F32))

    threshold = EXP_IS_ZERO_BELOW - 2.0 * qk_bound

    def count_cond(n):
        j = qi - 2 - n
        return jnp.logical_and(j >= 0, c0 - cend_ref[bh, jnp.maximum(j, 0)] >= threshold)

    n_blocks = lax.while_loop(count_cond, lambda n: n + 1, jnp.int32(0))

    def body(n, carry):
        m, l, acc = carry
        j = qi - 2 - n
        s = logits(j, c0)
        m_new = jnp.maximum(m, jnp.max(s, axis=-1, keepdims=True))
        p = jnp.exp2(s - m_new)
        alpha = jnp.exp2(m - m_new)
        l = alpha * l + jnp.sum(p, axis=-1, keepdims=True)
        acc = alpha * acc + jnp.dot(p.astype(BF16), values(j), preferred_element_type=F32)
        return m_new, l, acc

    m, l, acc = lax.fori_loop(0, n_blocks, body, (m, l, acc))
    o_ref[0] = (acc / l).astype(o_ref.dtype)


def _fox_attention(qkv, c, blk):
    _, batch, n_heads, seq, _ = qkv.shape
    nb = seq // blk
    cflat = c.reshape(batch * n_heads, nb, blk)
    cstart = cflat[:, :, 0]
    cend = cflat[:, :, blk - 1]
    cblocks = cflat.reshape(batch * n_heads, nb, 1, blk)
    kernel = functools.partial(_fox_kernel, n_heads=n_heads, blk=blk, scale=HEAD_DIM ** -0.5)
    grid_spec = pltpu.PrefetchScalarGridSpec(
        num_scalar_prefetch=2,
        grid=(batch, n_heads, nb),
        in_specs=[
            pl.BlockSpec((1, 1, 1, blk, HEAD_DIM), lambda b, h, i, *_: (0, b, h, i, 0)),
            pl.BlockSpec((1, 1, 1, seq, HEAD_DIM), lambda b, h, i, *_: (1, b, h, 0, 0)),
            pl.BlockSpec((1, 1, 1, seq, HEAD_DIM), lambda b, h, i, *_: (2, b, h, 0, 0)),
            pl.BlockSpec((1, nb, 1, blk), lambda b, h, i, *_: (b * n_heads + h, 0, 0, 0)),
        ],
        out_specs=pl.BlockSpec((1, blk, HEAD_DIM), lambda b, h, i, *_: (b, i, h)),
        scratch_shapes=[pltpu.SMEM((1,), F32)],
    )
    return pl.pallas_call(
        kernel,
        grid_spec=grid_spec,
        out_shape=jax.ShapeDtypeStruct((batch, seq, n_heads * HEAD_DIM), BF16),
        compiler_params=_params(("parallel", "parallel", "arbitrary")),
        name="fox_attention",
    )(cstart, cend, qkv, qkv, qkv, cblocks)


def _sb_kernel(q_ref, k_ref, v_ref, o_ref, *, blk, scale):
    qi = pl.program_id(2)
    q = (q_ref[0, 0, 0].astype(F32) * scale).astype(BF16)
    row = lax.broadcasted_iota(jnp.int32, (blk, blk), 0)
    col = lax.broadcasted_iota(jnp.int32, (blk, blk), 1)
    later = (row > col).astype(BF16)
    strict = col < row

    def block(j, carry, acc, masked, z_bias=None):
        kblk = k_ref[0, 0, 0, pl.ds(j * blk, blk), :]
        vblk = v_ref[0, 0, 0, pl.ds(j * blk, blk), :]
        z = lax.dot_general(q, kblk, (((1,), (1,)), ((), ())), preferred_element_type=F32)
        if z_bias is not None:
            z = z + z_bias
        log_not_beta = _log_sigmoid(-z)
        if masked:
            log_not_beta = jnp.where(strict, log_not_beta, 0.0)
        hi = log_not_beta.astype(BF16)
        lo = (log_not_beta - hi.astype(F32)).astype(BF16)
        after = (jnp.dot(hi, later, preferred_element_type=F32)
                 + jnp.dot(lo, later, preferred_element_type=F32) + carry)
        a = jnp.exp(z + log_not_beta + after)
        if masked:
            a = jnp.where(strict, a, 0.0)
        acc = acc + jnp.dot(a.astype(BF16), vblk, preferred_element_type=F32)
        carry = carry + jnp.sum(log_not_beta, axis=-1, keepdims=True)
        return carry, acc

    carry0 = jnp.zeros((blk, 1), F32)
    acc0 = jnp.zeros((blk, HEAD_DIM), F32)
    carry, acc = block(qi, carry0, acc0, True)
    carry, acc = block(jnp.maximum(qi - 1, 0), carry, acc, False,
                       z_bias=jnp.where(qi >= 1, 0.0, MASKED_LOGIT))

    def cond(state):
        j, carry, _ = state
        return jnp.logical_and(j >= 0, jnp.max(carry) >= EXP_IS_ZERO_BELOW)

    def body(state):
        j, carry, acc = state
        carry, acc = block(j, carry, acc, False)
        return j - 1, carry, acc

    _, _, acc = lax.while_loop(cond, body, (qi - 2, carry, acc))
    o_ref[0] = acc.astype(o_ref.dtype)


def _sb_attention(qkv, blk):
    _, batch, n_heads, seq, _ = qkv.shape
    nb = seq // blk
    kernel = functools.partial(_sb_kernel, blk=blk, scale=HEAD_DIM ** -0.5)
    return pl.pallas_call(
        kernel,
        grid=(batch, n_heads, nb),
        in_specs=[
            pl.BlockSpec((1, 1, 1, blk, HEAD_DIM), lambda b, h, i: (3, b, h, i, 0)),
            pl.BlockSpec((1, 1, 1, seq, HEAD_DIM), lambda b, h, i: (4, b, h, 0, 0)),
            pl.BlockSpec((1, 1, 1, seq, HEAD_DIM), lambda b, h, i: (5, b, h, 0, 0)),
        ],
        out_specs=pl.BlockSpec((1, blk, HEAD_DIM), lambda b, h, i: (b, i, h)),
        out_shape=jax.ShapeDtypeStruct((batch, seq, n_heads * HEAD_DIM), BF16),
        compiler_params=_params(("parallel", "parallel", "arbitrary")),
        name="sb_attention",
    )(qkv, qkv, qkv)


def _merge_kernel(ya_ref, yb_ref, ga_ref, gb_ref, x_ref, wa_ref, wb_ref, wo_ref, g_ref,
                  x2_ref, h_ref):
    ya = jnp.dot(ya_ref[...], wa_ref[...], preferred_element_type=F32)
    yb = jnp.dot(yb_ref[...], wb_ref[...], preferred_element_type=F32)
    merged = ga_ref[...].astype(F32) * ya + gb_ref[...].astype(F32) * yb
    x2 = x_ref[...] + jnp.dot(merged.astype(BF16), wo_ref[...], preferred_element_type=F32)
    x2_ref[...] = x2
    h_ref[...] = _rms_scale(x2, g_ref[...]).astype(h_ref.dtype)


def _merge_project(ya, yb, gates, x2d, wa, wb, wo, g, tm):
    t, d = x2d.shape
    wa_w = ya.shape[1]
    wb_w = yb.shape[1]
    resident = lambda shape: pl.BlockSpec(shape, lambda i: (0, 0), pipeline_mode=pl.Buffered(1))
    return pl.pallas_call(
        _merge_kernel,
        grid=(t // tm,),
        in_specs=[
            pl.BlockSpec((tm, wa_w), lambda i: (i, 0)),
            pl.BlockSpec((tm, wb_w), lambda i: (i, 0)),
            pl.BlockSpec((tm, d), lambda i: (i, 0)),
            pl.BlockSpec((tm, d), lambda i: (i, 1)),
            pl.BlockSpec((tm, d), lambda i: (i, 0)),
            resident((wa_w, d)),
            resident((wb_w, d)),
            resident((d, d)),
            resident((1, d)),
        ],
        out_specs=[
            pl.BlockSpec((tm, d), lambda i: (i, 0)),
            pl.BlockSpec((tm, d), lambda i: (i, 0)),
        ],
        out_shape=[
            jax.ShapeDtypeStruct((t, d), F32),
            jax.ShapeDtypeStruct((t, d), BF16),
        ],
        compiler_params=_params(("parallel",)),
        name="merge_project",
    )(ya, yb, gates, gates, x2d, wa, wb, wo, g)


def _mlp_kernel(h_ref, wu_ref, wd_ref, x2_ref, g_ref, o_ref, acc_ref, *, final_norm):
    j = pl.program_id(1)

    @pl.when(j == 0)
    def _():
        acc_ref[...] = x2_ref[...]

    u = jnp.maximum(jnp.dot(h_ref[...], wu_ref[...], preferred_element_type=F32), 0.0)
    acc_ref[...] += jnp.dot((u * u).astype(BF16), wd_ref[...], preferred_element_type=F32)

    @pl.when(j == pl.num_programs(1) - 1)
    def _():
        out = acc_ref[...]
        o_ref[...] = (_rms_scale(out, g_ref[...]) if final_norm else out).astype(o_ref.dtype)


def _mlp(h, wu, wd, x2, g, tm, tf, final_norm):
    t, d = x2.shape
    f = wu.shape[1]
    return pl.pallas_call(
        functools.partial(_mlp_kernel, final_norm=final_norm),
        grid=(t // tm, f // tf),
        in_specs=[
            pl.BlockSpec((tm, d), lambda i, j: (i, 0)),
            pl.BlockSpec((d, tf), lambda i, j: (0, j)),
            pl.BlockSpec((tf, d), lambda i, j: (j, 0)),
            pl.BlockSpec((tm, d), lambda i, j: (i, 0)),
            pl.BlockSpec((1, d), lambda i, j: (0, 0)),
        ],
        out_specs=pl.BlockSpec((tm, d), lambda i, j: (i, 0)),
        out_shape=jax.ShapeDtypeStruct((t, d), F32),
        scratch_shapes=[pltpu.VMEM((tm, d), F32)],
        compiler_params=_params(("parallel", "arbitrary")),
        name="mlp",
    )(h, wu, wd, x2, g)


def _tile(n, want):
    t = min(n, want)
    while n % t:
        t //= 2
    return t


def kernel(x, norm_mix_g, w_in, b_forget, w_out_fox, w_out_sb, w_out, norm_mlp_g, w_mlp_up,
           w_mlp_down, norm_final_g):
    batch, seq, d = x.shape
    depth = w_in.shape[0]
    n_heads_fox = b_forget.shape[-1]
    width_fox = w_out_fox.shape[1]
    width_sb = w_out_sb.shape[1]
    n_heads_sb = width_sb // HEAD_DIM
    assert width_fox == n_heads_fox * HEAD_DIM and n_heads_fox == n_heads_sb
    assert n_heads_fox <= 8 and seq % 512 == 0 and d % V7X_LANES == 0
    t = batch * seq
    x2d = x.reshape(t, d)

    for l in range(depth):
        w = w_in[l]
        o_f = 3 * width_fox
        o_sb = o_f + n_heads_fox
        o_g = o_sb + 3 * width_sb
        w_qkv = jnp.concatenate([w[:, :o_f], w[:, o_sb:o_g]], axis=1).astype(BF16)
        w_f = jnp.pad(w[:, o_f:o_sb], ((0, 0), (0, V7X_LANES - n_heads_fox))).astype(BF16)
        b_f = jnp.pad(b_forget[l], (0, V7X_LANES - n_heads_fox)).reshape(1, V7X_LANES)
        w_g = w[:, o_g:].astype(BF16)

        xn, lf = _norm_forget(x2d, norm_mix_g[l].reshape(1, d), w_f, b_f, _tile(t, 512))
        c = _forget_cumsum(lf.reshape(batch, seq, V7X_LANES), n_heads_fox, _tile(seq, 512))
        qkv = _proj_heads(xn, w_qkv, batch, n_heads_fox, _tile(seq, 1024))
        gates = _proj_gates(xn, w_g, _tile(t, 1024), _tile(2 * d, 1024))

        ya = _fox_attention(qkv, c, 512).reshape(t, width_fox)
        yb = _sb_attention(qkv, 256).reshape(t, width_sb)

        x2d, h = _merge_project(ya, yb, gates, x2d, w_out_fox[l].astype(BF16),
                                w_out_sb[l].astype(BF16), w_out[l].astype(BF16),
                                norm_mlp_g[l].reshape(1, d), _tile(t, 512))
        x2d = _mlp(h, w_mlp_up[l].astype(BF16), w_mlp_down[l].astype(BF16), x2d,
                   norm_final_g.reshape(1, d), _tile(t, 512), _tile(w_mlp_up.shape[2], 1024),
                   final_norm=(l == depth - 1))
    return x2d.reshape(batch, seq, d)
```

```python
import functools

import jax
import jax.numpy as jnp
from jax import lax
from jax.experimental import pallas as pl
from jax.experimental.pallas import tpu as pltpu

HEAD_DIM = 128
RMS_EPS = 1e-6
MASKED_LOGIT = -1e30
EXP_IS_ZERO_BELOW = -104.0
LOG2_E = 1.4426950408889634
V7X_LANES = 128
V7X_VMEM_LIMIT_BYTES = 60 * 1024 * 1024

F32 = jnp.float32
BF16 = jnp.bfloat16


def _params(semantics, vmem_bytes=V7X_VMEM_LIMIT_BYTES):
    return pltpu.CompilerParams(dimension_semantics=semantics, vmem_limit_bytes=vmem_bytes)


def _log_sigmoid(u):
    return jnp.minimum(u, 0.0) - jnp.log1p(jnp.exp(-jnp.abs(u)))


def _rms_scale(x, g):
    ms = jnp.mean(x * x, axis=-1, keepdims=True)
    return x * lax.rsqrt(ms + RMS_EPS) * g


def _norm_forget_kernel(x_ref, g_ref, wf_ref, bf_ref, xn_ref, lf_ref):
    xn = _rms_scale(x_ref[...], g_ref[...]).astype(BF16)
    xn_ref[...] = xn
    f = jnp.dot(xn, wf_ref[...], preferred_element_type=F32) + bf_ref[...]
    lf_ref[...] = _log_sigmoid(f)


def _norm_forget(x2d, g, wf, bf, tm):
    t, d = x2d.shape
    return pl.pallas_call(
        _norm_forget_kernel,
        grid=(t // tm,),
        in_specs=[
            pl.BlockSpec((tm, d), lambda i: (i, 0)),
            pl.BlockSpec((1, d), lambda i: (0, 0)),
            pl.BlockSpec((d, V7X_LANES), lambda i: (0, 0)),
            pl.BlockSpec((1, V7X_LANES), lambda i: (0, 0)),
        ],
        out_specs=[
            pl.BlockSpec((tm, d), lambda i: (i, 0)),
            pl.BlockSpec((tm, V7X_LANES), lambda i: (i, 0)),
        ],
        out_shape=[
            jax.ShapeDtypeStruct((t, d), BF16),
            jax.ShapeDtypeStruct((t, V7X_LANES), F32),
        ],
        compiler_params=_params(("parallel",)),
        name="norm_forget",
    )(x2d, g, wf, bf)


def _cumsum_kernel(lf_ref, c_ref, carry_ref, *, n_heads):
    @pl.when(pl.program_id(1) == 0)
    def _():
        carry_ref[...] = jnp.zeros_like(carry_ref)

    tc = lf_ref.shape[1]
    lft = lf_ref[0].T[:n_heads, :]
    row = lax.broadcasted_iota(jnp.int32, (tc, tc), 0)
    col = lax.broadcasted_iota(jnp.int32, (tc, tc), 1)
    upper = (row <= col).astype(BF16)
    cs = carry_ref[:, :1]
    rest = lft
    for _ in range(3):
        piece = rest.astype(BF16)
        cs = cs + jnp.dot(piece, upper, preferred_element_type=F32)
        rest = rest - piece.astype(F32)
    c_ref[0] = cs
    carry_ref[...] = jnp.broadcast_to(cs[:, tc - 1:tc], carry_ref.shape)


def _forget_cumsum(lf, n_heads, tc):
    b, s, _ = lf.shape
    return pl.pallas_call(
        functools.partial(_cumsum_kernel, n_heads=n_heads),
        grid=(b, s // tc),
        in_specs=[pl.BlockSpec((1, tc, V7X_LANES), lambda i, j: (i, j, 0))],
        out_specs=pl.BlockSpec((1, n_heads, tc), lambda i, j: (i, 0, j)),
        out_shape=jax.ShapeDtypeStruct((b, n_heads, s), F32),
        scratch_shapes=[pltpu.VMEM((n_heads, V7X_LANES), F32)],
        compiler_params=_params(("arbitrary", "arbitrary")),
        name="forget_cumsum",
    )(lf)


def _proj_heads_kernel(x_ref, wa_ref, wb_ref, o_ref, n2_ref, *, groups_a, norm_group):
    j = pl.program_id(1)

    def emit(w_ref):
        acc = jnp.dot(x_ref[...], w_ref[...], preferred_element_type=F32)
        for hh in range(o_ref.shape[2]):
            o_ref[0, 0, hh] = acc[:, hh * HEAD_DIM:(hh + 1) * HEAD_DIM].astype(o_ref.dtype)

    @pl.when(j < groups_a)
    def _():
        emit(wa_ref)

    @pl.when(j >= groups_a)
    def _():
        emit(wb_ref)

    @pl.when(j == norm_group)
    def _():
        for hh in range(o_ref.shape[2]):
            v = o_ref[0, 0, hh].astype(F32)
            rs = jnp.sum(v * v, axis=-1, keepdims=True)
            n2_ref[0, hh:hh + 1, :] = jnp.broadcast_to(jnp.max(rs, axis=0, keepdims=True),
                                                       (1, V7X_LANES))


def _proj_heads(xn, wa, wb, batch, n_heads, tm, norm_group):
    t, d = xn.shape
    tn = n_heads * HEAD_DIM
    ga, gb = wa.shape[1] // tn, wb.shape[1] // tn
    s = t // batch
    nst = s // tm
    return pl.pallas_call(
        functools.partial(_proj_heads_kernel, groups_a=ga, norm_group=norm_group),
        grid=(t // tm, ga + gb),
        in_specs=[
            pl.BlockSpec((tm, d), lambda i, j: (i, 0)),
            pl.BlockSpec((d, tn), lambda i, j: (0, jnp.minimum(j, ga - 1))),
            pl.BlockSpec((d, tn), lambda i, j: (0, jnp.maximum(j - ga, 0))),
        ],
        out_specs=[
            pl.BlockSpec((1, 1, n_heads, tm, HEAD_DIM),
                         lambda i, j: (j, i // nst, 0, i % nst, 0)),
            pl.BlockSpec((1, n_heads, V7X_LANES), lambda i, j: (i, 0, 0)),
        ],
        out_shape=[
            jax.ShapeDtypeStruct((ga + gb, batch, n_heads, s, HEAD_DIM), BF16),
            jax.ShapeDtypeStruct((t // tm, n_heads, V7X_LANES), F32),
        ],
        compiler_params=_params(("parallel", "arbitrary")),
        name="proj_heads",
    )(xn, wa, wb)


def _proj_gate_kernel(x_ref, w_ref, o_ref):
    acc = jnp.dot(x_ref[...], w_ref[...], preferred_element_type=F32)
    o_ref[...] = jax.nn.sigmoid(acc).astype(o_ref.dtype)


def _proj_gates(xn, w, tm, tn):
    t, d = xn.shape
    n = w.shape[1]
    return pl.pallas_call(
        _proj_gate_kernel,
        grid=(t // tm, n // tn),
        in_specs=[
            pl.BlockSpec((tm, d), lambda i, j: (i, 0)),
            pl.BlockSpec((d, tn), lambda i, j: (0, j)),
        ],
        out_specs=pl.BlockSpec((tm, tn), lambda i, j: (i, j)),
        out_shape=jax.ShapeDtypeStruct((t, n), BF16),
        compiler_params=_params(("parallel", "arbitrary")),
        name="proj_gates",
    )(xn, w)


def _fox_kernel(cstart_ref, cend_ref, kn2_ref, q_ref, k_ref, v_ref, c_ref, o_ref, *,
                n_heads, blk, scale):
    b, h, qi = pl.program_id(0), pl.program_id(1), pl.program_id(2)
    bh = b * n_heads + h

    q2 = (q_ref[0, 0, 0].astype(F32) * (scale * LOG2_E)).astype(BF16)
    q2f = q2.astype(F32)
    qn2 = jnp.max(jnp.sum(q2f * q2f, axis=-1))
    qk_bound = jnp.sqrt(qn2 * kn2_ref[bh]) * (1.001 / LOG2_E)
    c0 = cstart_ref[bh, qi]

    def logits(j, row_bias):
        kblk = k_ref[0, 0, 0, pl.ds(j * blk, blk), :]
        s = lax.dot_general(q2, kblk, (((1,), (1,)), ((), ())), preferred_element_type=F32)
        return s + (row_bias - c_ref[0, j]) * LOG2_E

    def values(j):
        return v_ref[0, 0, 0, pl.ds(j * blk, blk), :]

    j_prev = jnp.maximum(qi - 1, 0)
    s_prev = logits(j_prev, c0 + jnp.where(qi >= 1, 0.0, MASKED_LOGIT))
    row = lax.broadcasted_iota(jnp.int32, (blk, blk), 0)
    col = lax.broadcasted_iota(jnp.int32, (blk, blk), 1)
    s_diag = jnp.where(col <= row, logits(qi, c0), MASKED_LOGIT)
    m = jnp.maximum(jnp.max(s_prev, axis=-1, keepdims=True),
                    jnp.max(s_diag, axis=-1, keepdims=True))
    p_prev = jnp.exp2(s_prev - m)
    p_diag = jnp.exp2(s_diag - m)
    l = jnp.sum(p_prev, axis=-1, keepdims=True) + jnp.sum(p_diag, axis=-1, keepdims=True)
    acc = (jnp.dot(p_prev.astype(BF16), values(j_prev), preferred_element_type=F32)
           + jnp.dot(p_diag.astype(BF16), values(qi), preferred_element_type=F32))

    threshold = EXP_IS_ZERO_BELOW - 2.0 * qk_bound

    def count_cond(n):
        j = qi - 2 - n
        return jnp.logical_and(j >= 0, c0 - cend_ref[bh, jnp.maximum(j, 0)] >= threshold)

    n_blocks = lax.while_loop(count_cond, lambda n: n + 1, jnp.int32(0))

    def general_body(n, carry):
        m, l, acc = carry
        j = qi - 2 - n
        s = logits(j, c0)
        m_new = jnp.maximum(m, jnp.max(s, axis=-1, keepdims=True))
        p = jnp.exp2(s - m_new)
        alpha = jnp.exp2(m - m_new)
        l = alpha * l + jnp.sum(p, axis=-1, keepdims=True)
        acc = alpha * acc + jnp.dot(p.astype(BF16), values(j), preferred_element_type=F32)
        return m_new, l, acc

    def general_loop(_):
        return lax.fori_loop(0, n_blocks, general_body, (m, l, acc))[1:]

    def fixed_max_body(n, carry):
        l, acc = carry
        ja = qi - 2 - 2 * n
        jb = ja - 1
        b_ok = jnp.logical_and(jb >= 0, 2 * n + 1 < n_blocks)
        jb = jnp.maximum(jb, 0)
        p_a = jnp.exp2(logits(ja, c0) - m)
        p_b = jnp.exp2(logits(jb, c0 + jnp.where(b_ok, 0.0, MASKED_LOGIT)) - m)
        l = l + jnp.sum(p_a, axis=-1, keepdims=True) + jnp.sum(p_b, axis=-1, keepdims=True)
        acc = (acc + jnp.dot(p_a.astype(BF16), values(ja), preferred_element_type=F32)
               + jnp.dot(p_b.astype(BF16), values(jb), preferred_element_type=F32))
        return l, acc

    def fixed_max_loop(_):
        return lax.fori_loop(0, (n_blocks + 1) // 2, fixed_max_body, (l, acc))

    nearest_end = cend_ref[bh, jnp.maximum(qi - 2, 0)]
    max_is_final = (qk_bound + c0 - nearest_end) * LOG2_E <= jnp.min(m)
    l, acc = lax.cond(max_is_final, fixed_max_loop, general_loop, None)
    o_ref[0] = (acc / l).astype(o_ref.dtype)


def _fox_attention(qkv, c, kn2, blk):
    _, batch, n_heads, seq, _ = qkv.shape
    nb = seq // blk
    cflat = c.reshape(batch * n_heads, nb, blk)
    cstart = cflat[:, :, 0]
    cend = cflat[:, :, blk - 1]
    cblocks = cflat.reshape(batch * n_heads, nb, 1, blk)
    kernel = functools.partial(_fox_kernel, n_heads=n_heads, blk=blk, scale=HEAD_DIM ** -0.5)
    grid_spec = pltpu.PrefetchScalarGridSpec(
        num_scalar_prefetch=3,
        grid=(batch, n_heads, nb),
        in_specs=[
            pl.BlockSpec((1, 1, 1, blk, HEAD_DIM), lambda b, h, i, *_: (0, b, h, i, 0)),
            pl.BlockSpec((1, 1, 1, seq, HEAD_DIM), lambda b, h, i, *_: (1, b, h, 0, 0)),
            pl.BlockSpec((1, 1, 1, seq, HEAD_DIM), lambda b, h, i, *_: (2, b, h, 0, 0)),
            pl.BlockSpec((1, nb, 1, blk), lambda b, h, i, *_: (b * n_heads + h, 0, 0, 0)),
        ],
        out_specs=pl.BlockSpec((1, blk, HEAD_DIM), lambda b, h, i, *_: (b, i, h)),
    )
    return pl.pallas_call(
        kernel,
        grid_spec=grid_spec,
        out_shape=jax.ShapeDtypeStruct((batch, seq, n_heads * HEAD_DIM), BF16),
        compiler_params=_params(("parallel", "parallel", "arbitrary")),
        name="fox_attention",
    )(cstart, cend, kn2, qkv, qkv, qkv, cblocks)


def _sb_kernel(q_ref, k_ref, v_ref, o_ref, *, sub, scale):
    qi = pl.program_id(2)
    n_sub = q_ref.shape[3] // sub
    row = lax.broadcasted_iota(jnp.int32, (sub, sub), 0)
    col = lax.broadcasted_iota(jnp.int32, (sub, sub), 1)
    later = (row > col).astype(BF16)
    strict = col < row

    def block(y_q, j, carry, acc, masked, y_bias=None):
        kblk = k_ref[0, 0, 0, pl.ds(j * sub, sub), :]
        vblk = v_ref[0, 0, 0, pl.ds(j * sub, sub), :]
        y = lax.dot_general(y_q, kblk, (((1,), (1,)), ((), ())), preferred_element_type=F32)
        if y_bias is not None:
            y = y + y_bias
        log2_not_beta = jnp.minimum(y, 0.0) - jnp.log2(1.0 + jnp.exp2(-jnp.abs(y)))
        if masked:
            log2_not_beta = jnp.where(strict, log2_not_beta, 0.0)
        after = jnp.dot(log2_not_beta.astype(BF16), later, preferred_element_type=F32) + carry
        a = jnp.exp2(log2_not_beta - y + after)
        if masked:
            a = jnp.where(strict, a, 0.0)
        acc = acc + jnp.dot(a.astype(BF16), vblk, preferred_element_type=F32)
        carry = carry + jnp.sum(log2_not_beta, axis=-1, keepdims=True)
        return carry, acc

    states = []
    for t in range(n_sub):
        y_q = (q_ref[0, 0, 0, t * sub:(t + 1) * sub, :].astype(F32)
               * (-scale * LOG2_E)).astype(BF16)
        jd = qi * n_sub + t
        carry, acc = block(y_q, jd, jnp.zeros((sub, 1), F32), jnp.zeros((sub, HEAD_DIM), F32),
                           True)
        if t == 0:
            carry, acc = block(y_q, jnp.maximum(jd - 1, 0), carry, acc, False,
                               y_bias=jnp.where(jd >= 1, 0.0, -MASKED_LOGIT))
        else:
            carry, acc = block(y_q, jd - 1, carry, acc, False)
        states.append((y_q, jd, carry, acc))

    for t, (y_q, jd, carry, acc) in enumerate(states):
        def cond(state):
            j, carry, _ = state
            return jnp.logical_and(j >= 0, jnp.max(carry) >= EXP_IS_ZERO_BELOW * LOG2_E)

        def body(state, y_q=y_q):
            j, carry, acc = state
            carry, acc = block(y_q, j, carry, acc, False)
            return j - 1, carry, acc

        _, _, acc = lax.while_loop(cond, body, (jd - 2, carry, acc))
        o_ref[0, t * sub:(t + 1) * sub, :] = acc.astype(o_ref.dtype)


def _sb_attention(qkv, blk, sub):
    _, batch, n_heads, seq, _ = qkv.shape
    nb = seq // blk
    kernel = functools.partial(_sb_kernel, sub=sub, scale=HEAD_DIM ** -0.5)
    return pl.pallas_call(
        kernel,
        grid=(batch, n_heads, nb),
        in_specs=[
            pl.BlockSpec((1, 1, 1, blk, HEAD_DIM), lambda b, h, i: (3, b, h, i, 0)),
            pl.BlockSpec((1, 1, 1, seq, HEAD_DIM), lambda b, h, i: (4, b, h, 0, 0)),
            pl.BlockSpec((1, 1, 1, seq, HEAD_DIM), lambda b, h, i: (5, b, h, 0, 0)),
        ],
        out_specs=pl.BlockSpec((1, blk, HEAD_DIM), lambda b, h, i: (b, i, h)),
        out_shape=jax.ShapeDtypeStruct((batch, seq, n_heads * HEAD_DIM), BF16),
        compiler_params=_params(("parallel", "parallel", "arbitrary")),
        name="sb_attention",
    )(qkv, qkv, qkv)


def _merge_kernel(ya_ref, yb_ref, ga_ref, gb_ref, x_ref, wa_ref, wb_ref, wo_ref, g_ref,
                  x2_ref, h_ref):
    ya = jnp.dot(ya_ref[...], wa_ref[...], preferred_element_type=F32)
    yb = jnp.dot(yb_ref[...], wb_ref[...], preferred_element_type=F32)
    merged = ga_ref[...].astype(F32) * ya + gb_ref[...].astype(F32) * yb
    x2 = x_ref[...] + jnp.dot(merged.astype(BF16), wo_ref[...], preferred_element_type=F32)
    x2_ref[...] = x2
    h_ref[...] = _rms_scale(x2, g_ref[...]).astype(h_ref.dtype)


def _merge_project(ya, yb, gates, x2d, wa, wb, wo, g, tm):
    t, d = x2d.shape
    wa_w = ya.shape[1]
    wb_w = yb.shape[1]
    resident = lambda shape: pl.BlockSpec(shape, lambda i: (0, 0), pipeline_mode=pl.Buffered(1))
    return pl.pallas_call(
        _merge_kernel,
        grid=(t // tm,),
        in_specs=[
            pl.BlockSpec((tm, wa_w), lambda i: (i, 0)),
            pl.BlockSpec((tm, wb_w), lambda i: (i, 0)),
            pl.BlockSpec((tm, d), lambda i: (i, 0)),
            pl.BlockSpec((tm, d), lambda i: (i, 1)),
            pl.BlockSpec((tm, d), lambda i: (i, 0)),
            resident((wa_w, d)),
            resident((wb_w, d)),
            resident((d, d)),
            resident((1, d)),
        ],
        out_specs=[
            pl.BlockSpec((tm, d), lambda i: (i, 0)),
            pl.BlockSpec((tm, d), lambda i: (i, 0)),
        ],
        out_shape=[
            jax.ShapeDtypeStruct((t, d), F32),
            jax.ShapeDtypeStruct((t, d), BF16),
        ],
        compiler_params=_params(("parallel",)),
        name="merge_project",
    )(ya, yb, gates, gates, x2d, wa, wb, wo, g)


def _mlp_kernel(h_ref, wu_ref, wd_ref, x2_ref, g_ref, o_ref, acc_ref, *, final_norm):
    j = pl.program_id(1)

    @pl.when(j == 0)
    def _():
        acc_ref[...] = x2_ref[...]

    u = jnp.maximum(jnp.dot(h_ref[...], wu_ref[...], preferred_element_type=F32), 0.0)
    acc_ref[...] += jnp.dot((u * u).astype(BF16), wd_ref[...], preferred_element_type=F32)

    @pl.when(j == pl.num_programs(1) - 1)
    def _():
        out = acc_ref[...]
        o_ref[...] = (_rms_scale(out, g_ref[...]) if final_norm else out).astype(o_ref.dtype)


def _mlp(h, wu, wd, x2, g, tm, tf, final_norm):
    t, d = x2.shape
    f = wu.shape[1]
    return pl.pallas_call(
        functools.partial(_mlp_kernel, final_norm=final_norm),
        grid=(t // tm, f // tf),
        in_specs=[
            pl.BlockSpec((tm, d), lambda i, j: (i, 0)),
            pl.BlockSpec((d, tf), lambda i, j: (0, j)),
            pl.BlockSpec((tf, d), lambda i, j: (j, 0)),
            pl.BlockSpec((tm, d), lambda i, j: (i, 0)),
            pl.BlockSpec((1, d), lambda i, j: (0, 0)),
        ],
        out_specs=pl.BlockSpec((tm, d), lambda i, j: (i, 0)),
        out_shape=jax.ShapeDtypeStruct((t, d), F32),
        scratch_shapes=[pltpu.VMEM((tm, d), F32)],
        compiler_params=_params(("parallel", "arbitrary")),
        name="mlp",
    )(h, wu, wd, x2, g)


def _tile(n, want):
    t = min(n, want)
    while n % t:
        t //= 2
    return t


def kernel(x, norm_mix_g, w_in, b_forget, w_out_fox, w_out_sb, w_out, norm_mlp_g, w_mlp_up,
           w_mlp_down, norm_final_g):
    batch, seq, d = x.shape
    depth = w_in.shape[0]
    n_heads_fox = b_forget.shape[-1]
    width_fox = w_out_fox.shape[1]
    width_sb = w_out_sb.shape[1]
    n_heads_sb = width_sb // HEAD_DIM
    assert width_fox == n_heads_fox * HEAD_DIM and n_heads_fox == n_heads_sb
    assert n_heads_fox <= 8 and seq % 512 == 0 and d % V7X_LANES == 0
    t = batch * seq
    x2d = x.reshape(t, d)

    for l in range(depth):
        w = w_in[l]
        o_f = 3 * width_fox
        o_sb = o_f + n_heads_fox
        o_g = o_sb + 3 * width_sb
        w_qkv_fox = w[:, :o_f].astype(BF16)
        w_qkv_sb = w[:, o_sb:o_g].astype(BF16)
        w_f = jnp.pad(w[:, o_f:o_sb], ((0, 0), (0, V7X_LANES - n_heads_fox))).astype(BF16)
        b_f = jnp.pad(b_forget[l], (0, V7X_LANES - n_heads_fox)).reshape(1, V7X_LANES)
        w_g = w[:, o_g:].astype(BF16)

        xn, lf = _norm_forget(x2d, norm_mix_g[l].reshape(1, d), w_f, b_f, _tile(t, 512))
        c = _forget_cumsum(lf.reshape(batch, seq, V7X_LANES), n_heads_fox, _tile(seq, 512))
        qkv, k_norm2 = _proj_heads(xn, w_qkv_fox, w_qkv_sb, batch, n_heads_fox,
                                   _tile(seq, 1024), norm_group=1)
        kn2 = k_norm2[:, :, 0].reshape(batch, -1, n_heads_fox).max(axis=1).reshape(-1)
        gates = _proj_gates(xn, w_g, _tile(t, 1024), _tile(2 * d, 1024))

        ya = _fox_attention(qkv, c, kn2, 512).reshape(t, width_fox)
        yb = _sb_attention(qkv, 512, 256).reshape(t, width_sb)

        x2d, h = _merge_project(ya, yb, gates, x2d, w_out_fox[l].astype(BF16),
                                w_out_sb[l].astype(BF16), w_out[l].astype(BF16),
                                norm_mlp_g[l].reshape(1, d), _tile(t, 512))
        x2d = _mlp(h, w_mlp_up[l].astype(BF16), w_mlp_down[l].astype(BF16), x2d,
                   norm_final_g.reshape(1, d), _tile(t, 512), _tile(w_mlp_up.shape[2], 1024),
                   final_norm=(l == depth - 1))
    return x2d.reshape(batch, seq, d)
```

```python
import functools

import jax
import jax.numpy as jnp
from jax import lax
from jax.experimental import pallas as pl
from jax.experimental.pallas import tpu as pltpu

HEAD_DIM = 128
RMS_EPS = 1e-6
MASKED_LOGIT = -1e30
EXP_IS_ZERO_BELOW = -104.0
LOG2_E = 1.4426950408889634
V7X_LANES = 128
V7X_VMEM_LIMIT_BYTES = 60 * 1024 * 1024

F32 = jnp.float32
BF16 = jnp.bfloat16


def _params(semantics, vmem_bytes=V7X_VMEM_LIMIT_BYTES):
    return pltpu.CompilerParams(dimension_semantics=semantics, vmem_limit_bytes=vmem_bytes)


def _log_sigmoid(u):
    return jnp.minimum(u, 0.0) - jnp.log1p(jnp.exp(-jnp.abs(u)))


def _rms_scale(x, g):
    ms = jnp.mean(x * x, axis=-1, keepdims=True)
    return x * lax.rsqrt(ms + RMS_EPS) * g


def _norm_forget_kernel(x_ref, g_ref, wf_ref, bf_ref, xn_ref, lf_ref):
    xn = _rms_scale(x_ref[...], g_ref[...]).astype(BF16)
    xn_ref[...] = xn
    f = jnp.dot(xn, wf_ref[...], preferred_element_type=F32) + bf_ref[...]
    lf_ref[...] = _log_sigmoid(f)


def _norm_forget(x2d, g, wf, bf, tm):
    t, d = x2d.shape
    return pl.pallas_call(
        _norm_forget_kernel,
        grid=(t // tm,),
        in_specs=[
            pl.BlockSpec((tm, d), lambda i: (i, 0)),
            pl.BlockSpec((1, d), lambda i: (0, 0)),
            pl.BlockSpec((d, V7X_LANES), lambda i: (0, 0)),
            pl.BlockSpec((1, V7X_LANES), lambda i: (0, 0)),
        ],
        out_specs=[
            pl.BlockSpec((tm, d), lambda i: (i, 0)),
            pl.BlockSpec((tm, V7X_LANES), lambda i: (i, 0)),
        ],
        out_shape=[
            jax.ShapeDtypeStruct((t, d), BF16),
            jax.ShapeDtypeStruct((t, V7X_LANES), F32),
        ],
        compiler_params=_params(("parallel",)),
        name="norm_forget",
    )(x2d, g, wf, bf)


def _cumsum_kernel(lf_ref, c_ref, carry_ref, *, n_heads):
    @pl.when(pl.program_id(1) == 0)
    def _():
        carry_ref[...] = jnp.zeros_like(carry_ref)

    tc = lf_ref.shape[1]
    lft = lf_ref[0].T[:n_heads, :]
    row = lax.broadcasted_iota(jnp.int32, (tc, tc), 0)
    col = lax.broadcasted_iota(jnp.int32, (tc, tc), 1)
    upper = (row <= col).astype(BF16)
    cs = carry_ref[:, :1]
    rest = lft
    for _ in range(3):
        piece = rest.astype(BF16)
        cs = cs + jnp.dot(piece, upper, preferred_element_type=F32)
        rest = rest - piece.astype(F32)
    c_ref[0] = cs
    carry_ref[...] = jnp.broadcast_to(cs[:, tc - 1:tc], carry_ref.shape)


def _forget_cumsum(lf, n_heads, tc):
    b, s, _ = lf.shape
    return pl.pallas_call(
        functools.partial(_cumsum_kernel, n_heads=n_heads),
        grid=(b, s // tc),
        in_specs=[pl.BlockSpec((1, tc, V7X_LANES), lambda i, j: (i, j, 0))],
        out_specs=pl.BlockSpec((1, n_heads, tc), lambda i, j: (i, 0, j)),
        out_shape=jax.ShapeDtypeStruct((b, n_heads, s), F32),
        scratch_shapes=[pltpu.VMEM((n_heads, V7X_LANES), F32)],
        compiler_params=_params(("arbitrary", "arbitrary")),
        name="forget_cumsum",
    )(lf)


def _proj_heads_kernel(x_ref, wa_ref, wb_ref, o_ref, n2_ref, *, groups_a, norm_group):
    j = pl.program_id(1)

    def emit(w_ref):
        acc = jnp.dot(x_ref[...], w_ref[...], preferred_element_type=F32)
        for hh in range(o_ref.shape[2]):
            o_ref[0, 0, hh] = acc[:, hh * HEAD_DIM:(hh + 1) * HEAD_DIM].astype(o_ref.dtype)

    @pl.when(j < groups_a)
    def _():
        emit(wa_ref)

    @pl.when(j >= groups_a)
    def _():
        emit(wb_ref)

    @pl.when(j == norm_group)
    def _():
        for hh in range(o_ref.shape[2]):
            v = o_ref[0, 0, hh].astype(F32)
            rs = jnp.sum(v * v, axis=-1, keepdims=True)
            n2_ref[0, hh:hh + 1, :] = jnp.broadcast_to(jnp.max(rs, axis=0, keepdims=True),
                                                       (1, V7X_LANES))


def _proj_heads(xn, wa, wb, batch, n_heads, tm, norm_group):
    t, d = xn.shape
    tn = n_heads * HEAD_DIM
    ga, gb = wa.shape[1] // tn, wb.shape[1] // tn
    s = t // batch
    nst = s // tm
    return pl.pallas_call(
        functools.partial(_proj_heads_kernel, groups_a=ga, norm_group=norm_group),
        grid=(t // tm, ga + gb),
        in_specs=[
            pl.BlockSpec((tm, d), lambda i, j: (i, 0)),
            pl.BlockSpec((d, tn), lambda i, j: (0, jnp.minimum(j, ga - 1))),
            pl.BlockSpec((d, tn), lambda i, j: (0, jnp.maximum(j - ga, 0))),
        ],
        out_specs=[
            pl.BlockSpec((1, 1, n_heads, tm, HEAD_DIM),
                         lambda i, j: (j, i // nst, 0, i % nst, 0)),
            pl.BlockSpec((1, n_heads, V7X_LANES), lambda i, j: (i, 0, 0)),
        ],
        out_shape=[
            jax.ShapeDtypeStruct((ga + gb, batch, n_heads, s, HEAD_DIM), BF16),
            jax.ShapeDtypeStruct((t // tm, n_heads, V7X_LANES), F32),
        ],
        compiler_params=_params(("parallel", "arbitrary")),
        name="proj_heads",
    )(xn, wa, wb)


def _proj_gate_kernel(x_ref, w_ref, o_ref):
    acc = jnp.dot(x_ref[...], w_ref[...], preferred_element_type=F32)
    o_ref[...] = jax.nn.sigmoid(acc).astype(o_ref.dtype)


def _proj_gates(xn, w, tm, tn):
    t, d = xn.shape
    n = w.shape[1]
    return pl.pallas_call(
        _proj_gate_kernel,
        grid=(t // tm, n // tn),
        in_specs=[
            pl.BlockSpec((tm, d), lambda i, j: (i, 0)),
            pl.BlockSpec((d, tn), lambda i, j: (0, j)),
        ],
        out_specs=pl.BlockSpec((tm, tn), lambda i, j: (i, j)),
        out_shape=jax.ShapeDtypeStruct((t, n), BF16),
        compiler_params=_params(("parallel", "arbitrary")),
        name="proj_gates",
    )(xn, w)


def _fox_kernel(cstart_ref, cend_ref, kn2_ref, q_ref, k_ref, v_ref, c_ref, o_ref, *,
                n_heads, blk, scale):
    b, h, qi = pl.program_id(0), pl.program_id(1), pl.program_id(2)
    bh = b * n_heads + h

    q2 = (q_ref[0, 0, 0].astype(F32) * (scale * LOG2_E)).astype(BF16)
    q2f = q2.astype(F32)
    qn2 = jnp.max(jnp.sum(q2f * q2f, axis=-1))
    qk_bound = jnp.sqrt(qn2 * kn2_ref[bh]) * (1.001 / LOG2_E)
    c0 = cstart_ref[bh, qi]

    def logits_t(j, shift):
        kblk = k_ref[0, 0, 0, pl.ds(j * blk, blk), :]
        s = lax.dot_general(kblk, q2, (((1,), (1,)), ((), ())), preferred_element_type=F32)
        bias_row = (shift - c_ref[0, j]) * LOG2_E
        bias_col = jnp.broadcast_to(bias_row, (V7X_LANES, blk)).T
        return s + jnp.concatenate([bias_col] * (blk // V7X_LANES), axis=1)

    def weighted_values_t(p_t, j):
        vblk = v_ref[0, 0, 0, pl.ds(j * blk, blk), :]
        return lax.dot_general(vblk, p_t.astype(BF16), (((0,), (0,)), ((), ())),
                               preferred_element_type=F32)

    j_prev = jnp.maximum(qi - 1, 0)
    s_prev = logits_t(j_prev, c0 + jnp.where(qi >= 1, 0.0, MASKED_LOGIT))
    key = lax.broadcasted_iota(jnp.int32, (blk, blk), 0)
    query = lax.broadcasted_iota(jnp.int32, (blk, blk), 1)
    s_diag = jnp.where(key <= query, logits_t(qi, c0), MASKED_LOGIT)
    m = jnp.maximum(jnp.max(s_prev, axis=0, keepdims=True),
                    jnp.max(s_diag, axis=0, keepdims=True))
    p_prev = jnp.exp2(s_prev - m)
    p_diag = jnp.exp2(s_diag - m)
    l = jnp.sum(p_prev, axis=0, keepdims=True) + jnp.sum(p_diag, axis=0, keepdims=True)
    acc = weighted_values_t(p_prev, j_prev) + weighted_values_t(p_diag, qi)

    threshold = EXP_IS_ZERO_BELOW - 2.0 * qk_bound

    def count_cond(n):
        j = qi - 2 - n
        return jnp.logical_and(j >= 0, c0 - cend_ref[bh, jnp.maximum(j, 0)] >= threshold)

    n_blocks = lax.while_loop(count_cond, lambda n: n + 1, jnp.int32(0))

    def general_body(n, carry):
        m, l, acc = carry
        j = qi - 2 - n
        s = logits_t(j, c0)
        m_new = jnp.maximum(m, jnp.max(s, axis=0, keepdims=True))
        p = jnp.exp2(s - m_new)
        alpha = jnp.exp2(m - m_new)
        l = alpha * l + jnp.sum(p, axis=0, keepdims=True)
        acc = alpha * acc + weighted_values_t(p, j)
        return m_new, l, acc

    def general_loop(_):
        return lax.fori_loop(0, n_blocks, general_body, (m, l, acc))[1:]

    def fixed_max_body(n, carry):
        l, acc = carry
        ja = qi - 2 - 2 * n
        jb = ja - 1
        b_ok = jnp.logical_and(jb >= 0, 2 * n + 1 < n_blocks)
        jb = jnp.maximum(jb, 0)
        p_a = jnp.exp2(logits_t(ja, c0) - m)
        p_b = jnp.exp2(logits_t(jb, c0 + jnp.where(b_ok, 0.0, MASKED_LOGIT)) - m)
        l = l + jnp.sum(p_a, axis=0, keepdims=True) + jnp.sum(p_b, axis=0, keepdims=True)
        acc = acc + weighted_values_t(p_a, ja) + weighted_values_t(p_b, jb)
        return l, acc

    def fixed_max_loop(_):
        return lax.fori_loop(0, (n_blocks + 1) // 2, fixed_max_body, (l, acc))

    nearest_end = cend_ref[bh, jnp.maximum(qi - 2, 0)]
    max_is_final = (qk_bound + c0 - nearest_end) * LOG2_E <= jnp.min(m)
    l, acc = lax.cond(max_is_final, fixed_max_loop, general_loop, None)
    o_ref[0] = (acc / l).T.astype(o_ref.dtype)


def _fox_attention(qkv, c, kn2, blk):
    _, batch, n_heads, seq, _ = qkv.shape
    nb = seq // blk
    cflat = c.reshape(batch * n_heads, nb, blk)
    cstart = cflat[:, :, 0]
    cend = cflat[:, :, blk - 1]
    cblocks = cflat.reshape(batch * n_heads, nb, 1, blk)
    kernel = functools.partial(_fox_kernel, n_heads=n_heads, blk=blk, scale=HEAD_DIM ** -0.5)
    grid_spec = pltpu.PrefetchScalarGridSpec(
        num_scalar_prefetch=3,
        grid=(batch, n_heads, nb),
        in_specs=[
            pl.BlockSpec((1, 1, 1, blk, HEAD_DIM), lambda b, h, i, *_: (0, b, h, i, 0)),
            pl.BlockSpec((1, 1, 1, seq, HEAD_DIM), lambda b, h, i, *_: (1, b, h, 0, 0)),
            pl.BlockSpec((1, 1, 1, seq, HEAD_DIM), lambda b, h, i, *_: (2, b, h, 0, 0)),
            pl.BlockSpec((1, nb, 1, blk), lambda b, h, i, *_: (b * n_heads + h, 0, 0, 0)),
        ],
        out_specs=pl.BlockSpec((1, blk, HEAD_DIM), lambda b, h, i, *_: (b, i, h)),
    )
    return pl.pallas_call(
        kernel,
        grid_spec=grid_spec,
        out_shape=jax.ShapeDtypeStruct((batch, seq, n_heads * HEAD_DIM), BF16),
        compiler_params=_params(("parallel", "parallel", "arbitrary")),
        name="fox_attention",
    )(cstart, cend, kn2, qkv, qkv, qkv, cblocks)


def _sb_kernel(q_ref, k_ref, v_ref, o_ref, *, sub, scale):
    qi = pl.program_id(2)
    n_sub = q_ref.shape[3] // sub
    row = lax.broadcasted_iota(jnp.int32, (sub, sub), 0)
    col = lax.broadcasted_iota(jnp.int32, (sub, sub), 1)
    later = (row > col).astype(BF16)
    strict = col < row

    def block(y_q, j, carry, acc, masked, y_bias=None):
        kblk = k_ref[0, 0, 0, pl.ds(j * sub, sub), :]
        vblk = v_ref[0, 0, 0, pl.ds(j * sub, sub), :]
        y = lax.dot_general(y_q, kblk, (((1,), (1,)), ((), ())), preferred_element_type=F32)
        if y_bias is not None:
            y = y + y_bias
        log2_not_beta = jnp.minimum(y, 0.0) - jnp.log2(1.0 + jnp.exp2(-jnp.abs(y)))
        if masked:
            log2_not_beta = jnp.where(strict, log2_not_beta, 0.0)
        after = jnp.dot(log2_not_beta.astype(BF16), later, preferred_element_type=F32) + carry
        a = jnp.exp2(log2_not_beta - y + after)
        if masked:
            a = jnp.where(strict, a, 0.0)
        acc = acc + jnp.dot(a.astype(BF16), vblk, preferred_element_type=F32)
        carry = carry + jnp.sum(log2_not_beta, axis=-1, keepdims=True)
        return carry, acc

    states = []
    for t in range(n_sub):
        y_q = (q_ref[0, 0, 0, t * sub:(t + 1) * sub, :].astype(F32)
               * (-scale * LOG2_E)).astype(BF16)
        jd = qi * n_sub + t
        carry, acc = block(y_q, jd, jnp.zeros((sub, 1), F32), jnp.zeros((sub, HEAD_DIM), F32),
                           True)
        if t == 0:
            carry, acc = block(y_q, jnp.maximum(jd - 1, 0), carry, acc, False,
                               y_bias=jnp.where(jd >= 1, 0.0, -MASKED_LOGIT))
        else:
            carry, acc = block(y_q, jd - 1, carry, acc, False)
        states.append((y_q, jd, carry, acc))

    for t, (y_q, jd, carry, acc) in enumerate(states):
        def cond(state):
            j, carry, _ = state
            return jnp.logical_and(j >= 0, jnp.max(carry) >= EXP_IS_ZERO_BELOW * LOG2_E)

        def body(state, y_q=y_q):
            j, carry, acc = state
            carry, acc = block(y_q, j, carry, acc, False)
            return j - 1, carry, acc

        _, _, acc = lax.while_loop(cond, body, (jd - 2, carry, acc))
        o_ref[0, t * sub:(t + 1) * sub, :] = acc.astype(o_ref.dtype)


def _sb_attention(qkv, blk, sub):
    _, batch, n_heads, seq, _ = qkv.shape
    nb = seq // blk
    kernel = functools.partial(_sb_kernel, sub=sub, scale=HEAD_DIM ** -0.5)
    return pl.pallas_call(
        kernel,
        grid=(batch, n_heads, nb),
        in_specs=[
            pl.BlockSpec((1, 1, 1, blk, HEAD_DIM), lambda b, h, i: (3, b, h, i, 0)),
            pl.BlockSpec((1, 1, 1, seq, HEAD_DIM), lambda b, h, i: (4, b, h, 0, 0)),
            pl.BlockSpec((1, 1, 1, seq, HEAD_DIM), lambda b, h, i: (5, b, h, 0, 0)),
        ],
        out_specs=pl.BlockSpec((1, blk, HEAD_DIM), lambda b, h, i: (b, i, h)),
        out_shape=jax.ShapeDtypeStruct((batch, seq, n_heads * HEAD_DIM), BF16),
        compiler_params=_params(("parallel", "parallel", "arbitrary")),
        name="sb_attention",
    )(qkv, qkv, qkv)


def _merge_kernel(ya_ref, yb_ref, ga_ref, gb_ref, x_ref, wa_ref, wb_ref, wo_ref, g_ref,
                  x2_ref, h_ref):
    ya = jnp.dot(ya_ref[...], wa_ref[...], preferred_element_type=F32)
    yb = jnp.dot(yb_ref[...], wb_ref[...], preferred_element_type=F32)
    merged = ga_ref[...].astype(F32) * ya + gb_ref[...].astype(F32) * yb
    x2 = x_ref[...] + jnp.dot(merged.astype(BF16), wo_ref[...], preferred_element_type=F32)
    x2_ref[...] = x2
    h_ref[...] = _rms_scale(x2, g_ref[...]).astype(h_ref.dtype)


def _merge_project(ya, yb, gates, x2d, wa, wb, wo, g, tm):
    t, d = x2d.shape
    wa_w = ya.shape[1]
    wb_w = yb.shape[1]
    resident = lambda shape: pl.BlockSpec(shape, lambda i: (0, 0), pipeline_mode=pl.Buffered(1))
    return pl.pallas_call(
        _merge_kernel,
        grid=(t // tm,),
        in_specs=[
            pl.BlockSpec((tm, wa_w), lambda i: (i, 0)),
            pl.BlockSpec((tm, wb_w), lambda i: (i, 0)),
            pl.BlockSpec((tm, d), lambda i: (i, 0)),
            pl.BlockSpec((tm, d), lambda i: (i, 1)),
            pl.BlockSpec((tm, d), lambda i: (i, 0)),
            resident((wa_w, d)),
            resident((wb_w, d)),
            resident((d, d)),
            resident((1, d)),
        ],
        out_specs=[
            pl.BlockSpec((tm, d), lambda i: (i, 0)),
            pl.BlockSpec((tm, d), lambda i: (i, 0)),
        ],
        out_shape=[
            jax.ShapeDtypeStruct((t, d), F32),
            jax.ShapeDtypeStruct((t, d), BF16),
        ],
        compiler_params=_params(("parallel",)),
        name="merge_project",
    )(ya, yb, gates, gates, x2d, wa, wb, wo, g)


def _mlp_kernel(h_ref, wu_ref, wd_ref, x2_ref, g_ref, o_ref, acc_ref, *, final_norm):
    j = pl.program_id(1)

    @pl.when(j == 0)
    def _():
        acc_ref[...] = x2_ref[...]

    u = jnp.maximum(jnp.dot(h_ref[...], wu_ref[...], preferred_element_type=F32), 0.0)
    acc_ref[...] += jnp.dot((u * u).astype(BF16), wd_ref[...], preferred_element_type=F32)

    @pl.when(j == pl.num_programs(1) - 1)
    def _():
        out = acc_ref[...]
        o_ref[...] = (_rms_scale(out, g_ref[...]) if final_norm else out).astype(o_ref.dtype)


def _mlp(h, wu, wd, x2, g, tm, tf, final_norm):
    t, d = x2.shape
    f = wu.shape[1]
    return pl.pallas_call(
        functools.partial(_mlp_kernel, final_norm=final_norm),
        grid=(t // tm, f // tf),
        in_specs=[
            pl.BlockSpec((tm, d), lambda i, j: (i, 0)),
            pl.BlockSpec((d, tf), lambda i, j: (0, j)),
            pl.BlockSpec((tf, d), lambda i, j: (j, 0)),
            pl.BlockSpec((tm, d), lambda i, j: (i, 0)),
            pl.BlockSpec((1, d), lambda i, j: (0, 0)),
        ],
        out_specs=pl.BlockSpec((tm, d), lambda i, j: (i, 0)),
        out_shape=jax.ShapeDtypeStruct((t, d), F32),
        scratch_shapes=[pltpu.VMEM((tm, d), F32)],
        compiler_params=_params(("parallel", "arbitrary")),
        name="mlp",
    )(h, wu, wd, x2, g)


def _tile(n, want):
    t = min(n, want)
    while n % t:
        t //= 2
    return t


def kernel(x, norm_mix_g, w_in, b_forget, w_out_fox, w_out_sb, w_out, norm_mlp_g, w_mlp_up,
           w_mlp_down, norm_final_g):
    batch, seq, d = x.shape
    depth = w_in.shape[0]
    n_heads_fox = b_forget.shape[-1]
    width_fox = w_out_fox.shape[1]
    width_sb = w_out_sb.shape[1]
    n_heads_sb = width_sb // HEAD_DIM
    assert width_fox == n_heads_fox * HEAD_DIM and n_heads_fox == n_heads_sb
    assert n_heads_fox <= 8 and seq % 1024 == 0 and d % V7X_LANES == 0
    t = batch * seq
    x2d = x.reshape(t, d)

    for l in range(depth):
        w = w_in[l]
        o_f = 3 * width_fox
        o_sb = o_f + n_heads_fox
        o_g = o_sb + 3 * width_sb
        w_qkv_fox = w[:, :o_f].astype(BF16)
        w_qkv_sb = w[:, o_sb:o_g].astype(BF16)
        w_f = jnp.pad(w[:, o_f:o_sb], ((0, 0), (0, V7X_LANES - n_heads_fox))).astype(BF16)
        b_f = jnp.pad(b_forget[l], (0, V7X_LANES - n_heads_fox)).reshape(1, V7X_LANES)
        w_g = w[:, o_g:].astype(BF16)

        xn, lf = _norm_forget(x2d, norm_mix_g[l].reshape(1, d), w_f, b_f, _tile(t, 512))
        c = _forget_cumsum(lf.reshape(batch, seq, V7X_LANES), n_heads_fox, _tile(seq, 512))
        qkv, k_norm2 = _proj_heads(xn, w_qkv_fox, w_qkv_sb, batch, n_heads_fox,
                                   _tile(seq, 1024), norm_group=1)
        kn2 = k_norm2[:, :, 0].reshape(batch, -1, n_heads_fox).max(axis=1).reshape(-1)
        gates = _proj_gates(xn, w_g, _tile(t, 1024), _tile(2 * d, 1024))

        ya = _fox_attention(qkv, c, kn2, 512).reshape(t, width_fox)
        yb = _sb_attention(qkv, 1024, 256).reshape(t, width_sb)

        x2d, h = _merge_project(ya, yb, gates, x2d, w_out_fox[l].astype(BF16),
                                w_out_sb[l].astype(BF16), w_out[l].astype(BF16),
                                norm_mlp_g[l].reshape(1, d), _tile(t, 512))
        x2d = _mlp(h, w_mlp_up[l].astype(BF16), w_mlp_down[l].astype(BF16), x2d,
                   norm_final_g.reshape(1, d), _tile(t, 512), _tile(w_mlp_up.shape[2], 1024),
                   final_norm=(l == depth - 1))
    return x2d.reshape(batch, seq, d)
```

```python
import functools

import jax
import jax.numpy as jnp
from jax import lax
from jax.experimental import pallas as pl
from jax.experimental.pallas import tpu as pltpu

HEAD_DIM = 128
RMS_EPS = 1e-6
MASKED_LOGIT = -1e30
EXP_IS_ZERO_BELOW = -104.0
LOG2_E = 1.4426950408889634
V7X_LANES = 128
V7X_VMEM_LIMIT_BYTES = 60 * 1024 * 1024

F32 = jnp.float32
BF16 = jnp.bfloat16


def _params(semantics, vmem_bytes=V7X_VMEM_LIMIT_BYTES):
    return pltpu.CompilerParams(dimension_semantics=semantics, vmem_limit_bytes=vmem_bytes)


def _log_sigmoid(u):
    return jnp.minimum(u, 0.0) - jnp.log1p(jnp.exp(-jnp.abs(u)))


def _rms_scale(x, g):
    ms = jnp.mean(x * x, axis=-1, keepdims=True)
    return x * lax.rsqrt(ms + RMS_EPS) * g


def _norm_forget_kernel(x_ref, g_ref, wf_ref, bf_ref, xn_ref, lf_ref):
    xn = _rms_scale(x_ref[...], g_ref[...]).astype(BF16)
    xn_ref[...] = xn
    f = jnp.dot(xn, wf_ref[...], preferred_element_type=F32) + bf_ref[...]
    lf_ref[...] = _log_sigmoid(f)


def _norm_forget(x2d, g, wf, bf, tm):
    t, d = x2d.shape
    return pl.pallas_call(
        _norm_forget_kernel,
        grid=(t // tm,),
        in_specs=[
            pl.BlockSpec((tm, d), lambda i: (i, 0)),
            pl.BlockSpec((1, d), lambda i: (0, 0)),
            pl.BlockSpec((d, V7X_LANES), lambda i: (0, 0)),
            pl.BlockSpec((1, V7X_LANES), lambda i: (0, 0)),
        ],
        out_specs=[
            pl.BlockSpec((tm, d), lambda i: (i, 0)),
            pl.BlockSpec((tm, V7X_LANES), lambda i: (i, 0)),
        ],
        out_shape=[
            jax.ShapeDtypeStruct((t, d), BF16),
            jax.ShapeDtypeStruct((t, V7X_LANES), F32),
        ],
        compiler_params=_params(("parallel",)),
        name="norm_forget",
    )(x2d, g, wf, bf)


def _cumsum_kernel(lf_ref, c_ref, carry_ref, *, n_heads):
    @pl.when(pl.program_id(1) == 0)
    def _():
        carry_ref[...] = jnp.zeros_like(carry_ref)

    tc = lf_ref.shape[1]
    lft = lf_ref[0].T[:n_heads, :]
    row = lax.broadcasted_iota(jnp.int32, (tc, tc), 0)
    col = lax.broadcasted_iota(jnp.int32, (tc, tc), 1)
    upper = (row <= col).astype(BF16)
    cs = carry_ref[:, :1]
    rest = lft
    for _ in range(3):
        piece = rest.astype(BF16)
        cs = cs + jnp.dot(piece, upper, preferred_element_type=F32)
        rest = rest - piece.astype(F32)
    c_ref[0] = cs
    carry_ref[...] = jnp.broadcast_to(cs[:, tc - 1:tc], carry_ref.shape)


def _forget_cumsum(lf, n_heads, tc):
    b, s, _ = lf.shape
    return pl.pallas_call(
        functools.partial(_cumsum_kernel, n_heads=n_heads),
        grid=(b, s // tc),
        in_specs=[pl.BlockSpec((1, tc, V7X_LANES), lambda i, j: (i, j, 0))],
        out_specs=pl.BlockSpec((1, n_heads, tc), lambda i, j: (i, 0, j)),
        out_shape=jax.ShapeDtypeStruct((b, n_heads, s), F32),
        scratch_shapes=[pltpu.VMEM((n_heads, V7X_LANES), F32)],
        compiler_params=_params(("arbitrary", "arbitrary")),
        name="forget_cumsum",
    )(lf)


def _proj_heads_kernel(x_ref, wa_ref, wb_ref, o_ref, n2_ref, *, groups_a, norm_group):
    j = pl.program_id(1)

    def emit(w_ref):
        acc = jnp.dot(x_ref[...], w_ref[...], preferred_element_type=F32)
        for hh in range(o_ref.shape[2]):
            o_ref[0, 0, hh] = acc[:, hh * HEAD_DIM:(hh + 1) * HEAD_DIM].astype(o_ref.dtype)

    @pl.when(j < groups_a)
    def _():
        emit(wa_ref)

    @pl.when(j >= groups_a)
    def _():
        emit(wb_ref)

    @pl.when(j == norm_group)
    def _():
        for hh in range(o_ref.shape[2]):
            v = o_ref[0, 0, hh].astype(F32)
            rs = jnp.sum(v * v, axis=-1, keepdims=True)
            n2_ref[0, hh:hh + 1, :] = jnp.broadcast_to(jnp.max(rs, axis=0, keepdims=True),
                                                       (1, V7X_LANES))


def _proj_heads(xn, wa, wb, batch, n_heads, tm, norm_group):
    t, d = xn.shape
    tn = n_heads * HEAD_DIM
    ga, gb = wa.shape[1] // tn, wb.shape[1] // tn
    s = t // batch
    nst = s // tm
    return pl.pallas_call(
        functools.partial(_proj_heads_kernel, groups_a=ga, norm_group=norm_group),
        grid=(t // tm, ga + gb),
        in_specs=[
            pl.BlockSpec((tm, d), lambda i, j: (i, 0)),
            pl.BlockSpec((d, tn), lambda i, j: (0, jnp.minimum(j, ga - 1))),
            pl.BlockSpec((d, tn), lambda i, j: (0, jnp.maximum(j - ga, 0))),
        ],
        out_specs=[
            pl.BlockSpec((1, 1, n_heads, tm, HEAD_DIM),
                         lambda i, j: (j, i // nst, 0, i % nst, 0)),
            pl.BlockSpec((1, n_heads, V7X_LANES), lambda i, j: (i, 0, 0)),
        ],
        out_shape=[
            jax.ShapeDtypeStruct((ga + gb, batch, n_heads, s, HEAD_DIM), BF16),
            jax.ShapeDtypeStruct((t // tm, n_heads, V7X_LANES), F32),
        ],
        compiler_params=_params(("parallel", "arbitrary")),
        name="proj_heads",
    )(xn, wa, wb)


def _proj_gate_kernel(x_ref, w_ref, o_ref):
    acc = jnp.dot(x_ref[...], w_ref[...], preferred_element_type=F32)
    o_ref[...] = jax.nn.sigmoid(acc).astype(o_ref.dtype)


def _proj_gates(xn, w, tm, tn):
    t, d = xn.shape
    n = w.shape[1]
    return pl.pallas_call(
        _proj_gate_kernel,
        grid=(t // tm, n // tn),
        in_specs=[
            pl.BlockSpec((tm, d), lambda i, j: (i, 0)),
            pl.BlockSpec((d, tn), lambda i, j: (0, j)),
        ],
        out_specs=pl.BlockSpec((tm, tn), lambda i, j: (i, j)),
        out_shape=jax.ShapeDtypeStruct((t, n), BF16),
        compiler_params=_params(("parallel", "arbitrary")),
        name="proj_gates",
    )(xn, w)


def _fox_kernel(cstart_ref, cend_ref, kn2_ref, q_ref, k_ref, v_ref, c_ref, o_ref, *,
                n_heads, blk, scale):
    b, h, step = pl.program_id(0), pl.program_id(1), pl.program_id(2)
    bh = b * n_heads + h
    tiles = q_ref.shape[3] // blk
    key = lax.broadcasted_iota(jnp.int32, (blk, blk), 0)
    query = lax.broadcasted_iota(jnp.int32, (blk, blk), 1)
    ones_rows = jnp.ones((8, HEAD_DIM), BF16)

    def logits_t(q2, j, shift):
        kblk = k_ref[0, 0, 0, pl.ds(j * blk, blk), :]
        s = lax.dot_general(kblk, q2, (((1,), (1,)), ((), ())), preferred_element_type=F32)
        bias_row = (shift - c_ref[0, j]) * LOG2_E
        bias_col = jnp.broadcast_to(bias_row, (V7X_LANES, blk)).T
        return s + jnp.concatenate([bias_col] * (blk // V7X_LANES), axis=1)

    def weighted_values_t(p_t, j):
        vblk = v_ref[0, 0, 0, pl.ds(j * blk, blk), :]
        return lax.dot_general(vblk, p_t.astype(BF16), (((0,), (0,)), ((), ())),
                               preferred_element_type=F32)

    def near_pass(t):
        qi = step * tiles + t
        c0 = cstart_ref[bh, qi]
        q2 = (q_ref[0, 0, 0, t * blk:(t + 1) * blk, :].astype(F32)
              * (scale * LOG2_E)).astype(BF16)
        j_prev = jnp.maximum(qi - 1, 0)
        s_prev = logits_t(q2, j_prev, c0 + jnp.where(qi >= 1, 0.0, MASKED_LOGIT))
        s_diag = jnp.where(key <= query, logits_t(q2, qi, c0), MASKED_LOGIT)
        m = jnp.maximum(jnp.max(s_prev, axis=0, keepdims=True),
                        jnp.max(s_diag, axis=0, keepdims=True))
        p_prev = jnp.exp2(s_prev - m)
        p_diag = jnp.exp2(s_diag - m)
        l = jnp.sum(p_prev, axis=0, keepdims=True) + jnp.sum(p_diag, axis=0, keepdims=True)
        acc = weighted_values_t(p_prev, j_prev) + weighted_values_t(p_diag, qi)
        q2f = q2.astype(F32)
        qn2 = lax.dot_general(ones_rows, (q2f * q2f).astype(BF16), (((1,), (1,)), ((), ())),
                              preferred_element_type=F32)[:1]
        excess = jnp.max(jnp.sqrt(qn2 * (1.01 * kn2_ref[bh])) - m)
        return qi, c0, q2, m, l, acc, excess

    def far_pass(qi, c0, q2, m, l, acc, excess):
        def forget_gap(j):
            return (c0 - cend_ref[bh, jnp.maximum(j, 0)]) * LOG2_E

        def count_cond(n):
            j = qi - 2 - n
            return jnp.logical_and(j >= 0,
                                   excess + forget_gap(j) >= EXP_IS_ZERO_BELOW * LOG2_E)

        n_blocks = lax.while_loop(count_cond, lambda n: n + 1, jnp.int32(0))

        def general_body(n, carry):
            m, l, acc = carry
            j = qi - 2 - n
            s = logits_t(q2, j, c0)
            m_new = jnp.maximum(m, jnp.max(s, axis=0, keepdims=True))
            p = jnp.exp2(s - m_new)
            alpha = jnp.exp2(m - m_new)
            l = alpha * l + jnp.sum(p, axis=0, keepdims=True)
            acc = alpha * acc + weighted_values_t(p, j)
            return m_new, l, acc

        def general_loop(_):
            return lax.fori_loop(0, n_blocks, general_body, (m, l, acc))[1:]

        def fixed_max_body(n, carry):
            l, acc = carry
            ja = qi - 2 - 2 * n
            jb = ja - 1
            b_ok = jnp.logical_and(jb >= 0, 2 * n + 1 < n_blocks)
            jb = jnp.maximum(jb, 0)
            p_a = jnp.exp2(logits_t(q2, ja, c0) - m)
            p_b = jnp.exp2(logits_t(q2, jb, c0 + jnp.where(b_ok, 0.0, MASKED_LOGIT)) - m)
            l = l + jnp.sum(p_a, axis=0, keepdims=True) + jnp.sum(p_b, axis=0, keepdims=True)
            acc = acc + weighted_values_t(p_a, ja) + weighted_values_t(p_b, jb)
            return l, acc

        def fixed_max_loop(_):
            return lax.fori_loop(0, (n_blocks + 1) // 2, fixed_max_body, (l, acc))

        max_is_final = excess + forget_gap(qi - 2) <= 0.0
        l, acc = lax.cond(max_is_final, fixed_max_loop, general_loop, None)
        return (acc / l).T

    near = [near_pass(t) for t in range(tiles)]
    for t, state in enumerate(near):
        o_ref[0, t * blk:(t + 1) * blk, :] = far_pass(*state).astype(o_ref.dtype)


def _fox_attention(qkv, c, kn2, blk, tiles):
    _, batch, n_heads, seq, _ = qkv.shape
    nb = seq // blk
    rows = tiles * blk
    cflat = c.reshape(batch * n_heads, nb, blk)
    cstart = cflat[:, :, 0]
    cend = cflat[:, :, blk - 1]
    cblocks = cflat.reshape(batch * n_heads, nb, 1, blk)
    kernel = functools.partial(_fox_kernel, n_heads=n_heads, blk=blk, scale=HEAD_DIM ** -0.5)
    grid_spec = pltpu.PrefetchScalarGridSpec(
        num_scalar_prefetch=3,
        grid=(batch, n_heads, seq // rows),
        in_specs=[
            pl.BlockSpec((1, 1, 1, rows, HEAD_DIM), lambda b, h, i, *_: (0, b, h, i, 0)),
            pl.BlockSpec((1, 1, 1, seq, HEAD_DIM), lambda b, h, i, *_: (1, b, h, 0, 0)),
            pl.BlockSpec((1, 1, 1, seq, HEAD_DIM), lambda b, h, i, *_: (2, b, h, 0, 0)),
            pl.BlockSpec((1, nb, 1, blk), lambda b, h, i, *_: (b * n_heads + h, 0, 0, 0)),
        ],
        out_specs=pl.BlockSpec((1, rows, HEAD_DIM), lambda b, h, i, *_: (b, i, h)),
    )
    return pl.pallas_call(
        kernel,
        grid_spec=grid_spec,
        out_shape=jax.ShapeDtypeStruct((batch, seq, n_heads * HEAD_DIM), BF16),
        compiler_params=_params(("parallel", "parallel", "arbitrary")),
        name="fox_attention",
    )(cstart, cend, kn2, qkv, qkv, qkv, cblocks)


def _sb_kernel(q_ref, k_ref, v_ref, o_ref, *, sub, scale):
    qi = pl.program_id(2)
    n_sub = q_ref.shape[3] // sub
    row = lax.broadcasted_iota(jnp.int32, (sub, sub), 0)
    col = lax.broadcasted_iota(jnp.int32, (sub, sub), 1)
    later = (row > col).astype(BF16)
    strict = col < row

    def block(y_q, j, carry, acc, masked, y_bias=None):
        kblk = k_ref[0, 0, 0, pl.ds(j * sub, sub), :]
        vblk = v_ref[0, 0, 0, pl.ds(j * sub, sub), :]
        y = lax.dot_general(y_q, kblk, (((1,), (1,)), ((), ())), preferred_element_type=F32)
        if y_bias is not None:
            y = y + y_bias
        log2_not_beta = jnp.minimum(y, 0.0) - jnp.log2(1.0 + jnp.exp2(-jnp.abs(y)))
        if masked:
            log2_not_beta = jnp.where(strict, log2_not_beta, 0.0)
        after = jnp.dot(log2_not_beta.astype(BF16), later, preferred_element_type=F32) + carry
        a = jnp.exp2(log2_not_beta - y + after)
        if masked:
            a = jnp.where(strict, a, 0.0)
        acc = acc + jnp.dot(a.astype(BF16), vblk, preferred_element_type=F32)
        carry = carry + jnp.sum(log2_not_beta, axis=-1, keepdims=True)
        return carry, acc

    states = []
    for t in range(n_sub):
        y_q = (q_ref[0, 0, 0, t * sub:(t + 1) * sub, :].astype(F32)
               * (-scale * LOG2_E)).astype(BF16)
        jd = qi * n_sub + t
        carry, acc = block(y_q, jd, jnp.zeros((sub, 1), F32), jnp.zeros((sub, HEAD_DIM), F32),
                           True)
        if t == 0:
            carry, acc = block(y_q, jnp.maximum(jd - 1, 0), carry, acc, False,
                               y_bias=jnp.where(jd >= 1, 0.0, -MASKED_LOGIT))
        else:
            carry, acc = block(y_q, jd - 1, carry, acc, False)
        states.append((y_q, jd, carry, acc))

    def walk_back(_):
        accs = []
        for y_q, jd, carry, acc in states:
            def cond(state):
                j, carry, _ = state
                return jnp.logical_and(j >= 0, jnp.max(carry) >= EXP_IS_ZERO_BELOW * LOG2_E)

            def body(state, y_q=y_q):
                j, carry, acc = state
                carry, acc = block(y_q, j, carry, acc, False)
                return j - 1, carry, acc

            accs.append(lax.while_loop(cond, body, (jd - 2, carry, acc))[2])
        return accs

    highest_carry = jnp.max(functools.reduce(jnp.maximum, [s[2] for s in states]))
    accs = lax.cond(highest_carry >= EXP_IS_ZERO_BELOW * LOG2_E, walk_back,
                    lambda _: [s[3] for s in states], None)
    for t, acc in enumerate(accs):
        o_ref[0, t * sub:(t + 1) * sub, :] = acc.astype(o_ref.dtype)


def _sb_attention(qkv, blk, sub):
    _, batch, n_heads, seq, _ = qkv.shape
    nb = seq // blk
    kernel = functools.partial(_sb_kernel, sub=sub, scale=HEAD_DIM ** -0.5)
    return pl.pallas_call(
        kernel,
        grid=(batch, n_heads, nb),
        in_specs=[
            pl.BlockSpec((1, 1, 1, blk, HEAD_DIM), lambda b, h, i: (3, b, h, i, 0)),
            pl.BlockSpec((1, 1, 1, seq, HEAD_DIM), lambda b, h, i: (4, b, h, 0, 0)),
            pl.BlockSpec((1, 1, 1, seq, HEAD_DIM), lambda b, h, i: (5, b, h, 0, 0)),
        ],
        out_specs=pl.BlockSpec((1, blk, HEAD_DIM), lambda b, h, i: (b, i, h)),
        out_shape=jax.ShapeDtypeStruct((batch, seq, n_heads * HEAD_DIM), BF16),
        compiler_params=_params(("parallel", "parallel", "arbitrary")),
        name="sb_attention",
    )(qkv, qkv, qkv)


def _merge_kernel(ya_ref, yb_ref, ga_ref, gb_ref, x_ref, wa_ref, wb_ref, wo_ref, g_ref,
                  x2_ref, h_ref):
    ya = jnp.dot(ya_ref[...], wa_ref[...], preferred_element_type=F32)
    yb = jnp.dot(yb_ref[...], wb_ref[...], preferred_element_type=F32)
    merged = ga_ref[...].astype(F32) * ya + gb_ref[...].astype(F32) * yb
    x2 = x_ref[...] + jnp.dot(merged.astype(BF16), wo_ref[...], preferred_element_type=F32)
    x2_ref[...] = x2
    h_ref[...] = _rms_scale(x2, g_ref[...]).astype(h_ref.dtype)


def _merge_project(ya, yb, gates, x2d, wa, wb, wo, g, tm):
    t, d = x2d.shape
    wa_w = ya.shape[1]
    wb_w = yb.shape[1]
    resident = lambda shape: pl.BlockSpec(shape, lambda i: (0, 0), pipeline_mode=pl.Buffered(1))
    return pl.pallas_call(
        _merge_kernel,
        grid=(t // tm,),
        in_specs=[
            pl.BlockSpec((tm, wa_w), lambda i: (i, 0)),
            pl.BlockSpec((tm, wb_w), lambda i: (i, 0)),
            pl.BlockSpec((tm, d), lambda i: (i, 0)),
            pl.BlockSpec((tm, d), lambda i: (i, 1)),
            pl.BlockSpec((tm, d), lambda i: (i, 0)),
            resident((wa_w, d)),
            resident((wb_w, d)),
            resident((d, d)),
            resident((1, d)),
        ],
        out_specs=[
            pl.BlockSpec((tm, d), lambda i: (i, 0)),
            pl.BlockSpec((tm, d), lambda i: (i, 0)),
        ],
        out_shape=[
            jax.ShapeDtypeStruct((t, d), F32),
            jax.ShapeDtypeStruct((t, d), BF16),
        ],
        compiler_params=_params(("parallel",)),
        name="merge_project",
    )(ya, yb, gates, gates, x2d, wa, wb, wo, g)


def _mlp_kernel(h_ref, wu_ref, wd_ref, x2_ref, g_ref, o_ref, acc_ref, *, final_norm):
    j = pl.program_id(1)

    @pl.when(j == 0)
    def _():
        acc_ref[...] = x2_ref[...]

    u = jnp.maximum(jnp.dot(h_ref[...], wu_ref[...], preferred_element_type=F32), 0.0)
    acc_ref[...] += jnp.dot((u * u).astype(BF16), wd_ref[...], preferred_element_type=F32)

    @pl.when(j == pl.num_programs(1) - 1)
    def _():
        out = acc_ref[...]
        o_ref[...] = (_rms_scale(out, g_ref[...]) if final_norm else out).astype(o_ref.dtype)


def _mlp(h, wu, wd, x2, g, tm, tf, final_norm):
    t, d = x2.shape
    f = wu.shape[1]
    return pl.pallas_call(
        functools.partial(_mlp_kernel, final_norm=final_norm),
        grid=(t // tm, f // tf),
        in_specs=[
            pl.BlockSpec((tm, d), lambda i, j: (i, 0)),
            pl.BlockSpec((d, tf), lambda i, j: (0, j)),
            pl.BlockSpec((tf, d), lambda i, j: (j, 0)),
            pl.BlockSpec((tm, d), lambda i, j: (i, 0)),
            pl.BlockSpec((1, d), lambda i, j: (0, 0)),
        ],
        out_specs=pl.BlockSpec((tm, d), lambda i, j: (i, 0)),
        out_shape=jax.ShapeDtypeStruct((t, d), F32),
        scratch_shapes=[pltpu.VMEM((tm, d), F32)],
        compiler_params=_params(("parallel", "arbitrary")),
        name="mlp",
    )(h, wu, wd, x2, g)


def _tile(n, want):
    t = min(n, want)
    while n % t:
        t //= 2
    return t


def kernel(x, norm_mix_g, w_in, b_forget, w_out_fox, w_out_sb, w_out, norm_mlp_g, w_mlp_up,
           w_mlp_down, norm_final_g):
    batch, seq, d = x.shape
    depth = w_in.shape[0]
    n_heads_fox = b_forget.shape[-1]
    width_fox = w_out_fox.shape[1]
    width_sb = w_out_sb.shape[1]
    n_heads_sb = width_sb // HEAD_DIM
    assert width_fox == n_heads_fox * HEAD_DIM and n_heads_fox == n_heads_sb
    assert n_heads_fox <= 8 and seq % 1024 == 0 and d % V7X_LANES == 0
    t = batch * seq
    x2d = x.reshape(t, d)

    for l in range(depth):
        w = w_in[l]
        o_f = 3 * width_fox
        o_sb = o_f + n_heads_fox
        o_g = o_sb + 3 * width_sb
        w_qkv_fox = w[:, :o_f].astype(BF16)
        w_qkv_sb = w[:, o_sb:o_g].astype(BF16)
        w_f = jnp.pad(w[:, o_f:o_sb], ((0, 0), (0, V7X_LANES - n_heads_fox))).astype(BF16)
        b_f = jnp.pad(b_forget[l], (0, V7X_LANES - n_heads_fox)).reshape(1, V7X_LANES)
        w_g = w[:, o_g:].astype(BF16)

        xn, lf = _norm_forget(x2d, norm_mix_g[l].reshape(1, d), w_f, b_f, _tile(t, 512))
        c = _forget_cumsum(lf.reshape(batch, seq, V7X_LANES), n_heads_fox, _tile(seq, 512))
        qkv, k_norm2 = _proj_heads(xn, w_qkv_fox, w_qkv_sb, batch, n_heads_fox,
                                   _tile(seq, 1024), norm_group=1)
        kn2 = k_norm2[:, :, 0].reshape(batch, -1, n_heads_fox).max(axis=1).reshape(-1)
        gates = _proj_gates(xn, w_g, _tile(t, 1024), _tile(2 * d, 1024))

        ya = _fox_attention(qkv, c, kn2, 512, 2).reshape(t, width_fox)
        yb = _sb_attention(qkv, 1024, 256).reshape(t, width_sb)

        x2d, h = _merge_project(ya, yb, gates, x2d, w_out_fox[l].astype(BF16),
                                w_out_sb[l].astype(BF16), w_out[l].astype(BF16),
                                norm_mlp_g[l].reshape(1, d), _tile(t, 512))
        x2d = _mlp(h, w_mlp_up[l].astype(BF16), w_mlp_down[l].astype(BF16), x2d,
                   norm_final_g.reshape(1, d), _tile(t, 512), _tile(w_mlp_up.shape[2], 1024),
                   final_norm=(l == depth - 1))
    return x2d.reshape(batch, seq, d)
```

```python
import functools

import jax
import jax.numpy as jnp
from jax import lax
from jax.experimental import pallas as pl
from jax.experimental.pallas import tpu as pltpu

HEAD_DIM = 128
RMS_EPS = 1e-6
MASKED_LOGIT = -1e30
EXP_IS_ZERO_BELOW = -104.0
LOG2_E = 1.4426950408889634
V7X_LANES = 128
V7X_VMEM_LIMIT_BYTES = 60 * 1024 * 1024

F32 = jnp.float32
BF16 = jnp.bfloat16


def _params(semantics, vmem_bytes=V7X_VMEM_LIMIT_BYTES):
    return pltpu.CompilerParams(dimension_semantics=semantics, vmem_limit_bytes=vmem_bytes)


def _log_sigmoid(u):
    return jnp.minimum(u, 0.0) - jnp.log1p(jnp.exp(-jnp.abs(u)))


def _rms_scale(x, g):
    ms = jnp.mean(x * x, axis=-1, keepdims=True)
    return x * lax.rsqrt(ms + RMS_EPS) * g


def _norm_forget_kernel(x_ref, g_ref, wf_ref, bf_ref, xn_ref, lf_ref):
    xn = _rms_scale(x_ref[...], g_ref[...]).astype(BF16)
    xn_ref[...] = xn
    f = jnp.dot(xn, wf_ref[...], preferred_element_type=F32) + bf_ref[...]
    lf_ref[...] = _log_sigmoid(f)


def _norm_forget(x2d, g, wf, bf, tm):
    t, d = x2d.shape
    return pl.pallas_call(
        _norm_forget_kernel,
        grid=(t // tm,),
        in_specs=[
            pl.BlockSpec((tm, d), lambda i: (i, 0)),
            pl.BlockSpec((1, d), lambda i: (0, 0)),
            pl.BlockSpec((d, V7X_LANES), lambda i: (0, 0)),
            pl.BlockSpec((1, V7X_LANES), lambda i: (0, 0)),
        ],
        out_specs=[
            pl.BlockSpec((tm, d), lambda i: (i, 0)),
            pl.BlockSpec((tm, V7X_LANES), lambda i: (i, 0)),
        ],
        out_shape=[
            jax.ShapeDtypeStruct((t, d), BF16),
            jax.ShapeDtypeStruct((t, V7X_LANES), F32),
        ],
        compiler_params=_params(("parallel",)),
        name="norm_forget",
    )(x2d, g, wf, bf)


def _cumsum_kernel(lf_ref, c_ref, carry_ref, *, n_heads):
    @pl.when(pl.program_id(1) == 0)
    def _():
        carry_ref[...] = jnp.zeros_like(carry_ref)

    tc = lf_ref.shape[1]
    lft = lf_ref[0].T[:n_heads, :]
    row = lax.broadcasted_iota(jnp.int32, (tc, tc), 0)
    col = lax.broadcasted_iota(jnp.int32, (tc, tc), 1)
    upper = (row <= col).astype(BF16)
    cs = carry_ref[:, :1]
    rest = lft
    for _ in range(3):
        piece = rest.astype(BF16)
        cs = cs + jnp.dot(piece, upper, preferred_element_type=F32)
        rest = rest - piece.astype(F32)
    c_ref[0] = cs
    carry_ref[...] = jnp.broadcast_to(cs[:, tc - 1:tc], carry_ref.shape)


def _forget_cumsum(lf, n_heads, tc):
    b, s, _ = lf.shape
    return pl.pallas_call(
        functools.partial(_cumsum_kernel, n_heads=n_heads),
        grid=(b, s // tc),
        in_specs=[pl.BlockSpec((1, tc, V7X_LANES), lambda i, j: (i, j, 0))],
        out_specs=pl.BlockSpec((1, n_heads, tc), lambda i, j: (i, 0, j)),
        out_shape=jax.ShapeDtypeStruct((b, n_heads, s), F32),
        scratch_shapes=[pltpu.VMEM((n_heads, V7X_LANES), F32)],
        compiler_params=_params(("arbitrary", "arbitrary")),
        name="forget_cumsum",
    )(lf)


def _proj_heads_kernel(x_ref, wa_ref, wb_ref, o_ref, *, groups_a):
    j = pl.program_id(1)

    def emit(w_ref):
        acc = jnp.dot(x_ref[...], w_ref[...], preferred_element_type=F32)
        for hh in range(o_ref.shape[2]):
            o_ref[0, 0, hh] = acc[:, hh * HEAD_DIM:(hh + 1) * HEAD_DIM].astype(o_ref.dtype)

    @pl.when(j < groups_a)
    def _():
        emit(wa_ref)

    @pl.when(j >= groups_a)
    def _():
        emit(wb_ref)


def _proj_heads(xn, wa, wb, batch, n_heads, tm):
    t, d = xn.shape
    tn = n_heads * HEAD_DIM
    ga, gb = wa.shape[1] // tn, wb.shape[1] // tn
    s = t // batch
    nst = s // tm
    return pl.pallas_call(
        functools.partial(_proj_heads_kernel, groups_a=ga),
        grid=(t // tm, ga + gb),
        in_specs=[
            pl.BlockSpec((tm, d), lambda i, j: (i, 0)),
            pl.BlockSpec((d, tn), lambda i, j: (0, jnp.minimum(j, ga - 1))),
            pl.BlockSpec((d, tn), lambda i, j: (0, jnp.maximum(j - ga, 0))),
        ],
        out_specs=pl.BlockSpec((1, 1, n_heads, tm, HEAD_DIM),
                               lambda i, j: (j, i // nst, 0, i % nst, 0)),
        out_shape=jax.ShapeDtypeStruct((ga + gb, batch, n_heads, s, HEAD_DIM), BF16),
        compiler_params=_params(("parallel", "arbitrary")),
        name="proj_heads",
    )(xn, wa, wb)


def _proj_gate_kernel(x_ref, w_ref, o_ref):
    acc = jnp.dot(x_ref[...], w_ref[...], preferred_element_type=F32)
    o_ref[...] = (0.5 * jnp.tanh(0.5 * acc) + 0.5).astype(o_ref.dtype)


def _proj_gates(xn, w, tm, tn):
    t, d = xn.shape
    n = w.shape[1]
    return pl.pallas_call(
        _proj_gate_kernel,
        grid=(t // tm, n // tn),
        in_specs=[
            pl.BlockSpec((tm, d), lambda i, j: (i, 0)),
            pl.BlockSpec((d, tn), lambda i, j: (0, j)),
        ],
        out_specs=pl.BlockSpec((tm, tn), lambda i, j: (i, j)),
        out_shape=jax.ShapeDtypeStruct((t, n), BF16),
        compiler_params=_params(("parallel", "arbitrary")),
        name="proj_gates",
    )(xn, w)


def _fox_kernel(cstart_ref, cend_ref, q_ref, k_ref, v_ref, c_ref, o_ref, kn2_ref, *,
                n_heads, blk, scale):
    b, h, step = pl.program_id(0), pl.program_id(1), pl.program_id(2)
    bh = b * n_heads + h
    tiles = q_ref.shape[3] // blk
    key = lax.broadcasted_iota(jnp.int32, (blk, blk), 0)
    query = lax.broadcasted_iota(jnp.int32, (blk, blk), 1)
    ones_rows = jnp.ones((8, HEAD_DIM), BF16)

    @pl.when(step == 0)
    def _():
        kn2_ref[0] = jnp.float32(0.0)

    kf = k_ref[0, 0, 0, pl.ds(step * (tiles * blk), tiles * blk), :].astype(F32)
    kn2_ref[0] = jnp.maximum(kn2_ref[0], jnp.max(jnp.sum(kf * kf, axis=-1)))
    kn2 = kn2_ref[0]

    def logits_t(q2, j, shift):
        kblk = k_ref[0, 0, 0, pl.ds(j * blk, blk), :]
        s = lax.dot_general(kblk, q2, (((1,), (1,)), ((), ())), preferred_element_type=F32)
        bias_row = (shift - c_ref[0, j]) * LOG2_E
        bias_col = jnp.broadcast_to(bias_row, (V7X_LANES, blk)).T
        return s + jnp.concatenate([bias_col] * (blk // V7X_LANES), axis=1)

    def weighted_values_t(p_t, j):
        vblk = v_ref[0, 0, 0, pl.ds(j * blk, blk), :]
        return lax.dot_general(vblk, p_t.astype(BF16), (((0,), (0,)), ((), ())),
                               preferred_element_type=F32)

    def near_pass(t):
        qi = step * tiles + t
        c0 = cstart_ref[bh, qi]
        q2 = (q_ref[0, 0, 0, t * blk:(t + 1) * blk, :].astype(F32)
              * (scale * LOG2_E)).astype(BF16)
        j_prev = jnp.maximum(qi - 1, 0)
        s_prev = logits_t(q2, j_prev, c0 + jnp.where(qi >= 1, 0.0, MASKED_LOGIT))
        s_diag = jnp.where(key <= query, logits_t(q2, qi, c0), MASKED_LOGIT)
        m = jnp.maximum(jnp.max(s_prev, axis=0, keepdims=True),
                        jnp.max(s_diag, axis=0, keepdims=True))
        p_prev = jnp.exp2(s_prev - m)
        p_diag = jnp.exp2(s_diag - m)
        l = jnp.sum(p_prev, axis=0, keepdims=True) + jnp.sum(p_diag, axis=0, keepdims=True)
        acc = weighted_values_t(p_prev, j_prev) + weighted_values_t(p_diag, qi)
        q2f = q2.astype(F32)
        qn2 = lax.dot_general(ones_rows, (q2f * q2f).astype(BF16), (((1,), (1,)), ((), ())),
                              preferred_element_type=F32)[:1]
        excess = jnp.max(jnp.sqrt(qn2 * (1.01 * kn2)) - m)
        return qi, c0, q2, m, l, acc, excess

    def far_pass(qi, c0, q2, m, l, acc, excess):
        def forget_gap(j):
            return (c0 - cend_ref[bh, jnp.maximum(j, 0)]) * LOG2_E

        def count_cond(n):
            j = qi - 2 - n
            return jnp.logical_and(j >= 0,
                                   excess + forget_gap(j) >= EXP_IS_ZERO_BELOW * LOG2_E)

        n_blocks = lax.while_loop(count_cond, lambda n: n + 1, jnp.int32(0))

        def general_body(n, carry):
            m, l, acc = carry
            j = qi - 2 - n
            s = logits_t(q2, j, c0)
            m_new = jnp.maximum(m, jnp.max(s, axis=0, keepdims=True))
            p = jnp.exp2(s - m_new)
            alpha = jnp.exp2(m - m_new)
            l = alpha * l + jnp.sum(p, axis=0, keepdims=True)
            acc = alpha * acc + weighted_values_t(p, j)
            return m_new, l, acc

        def general_loop(_):
            return lax.fori_loop(0, n_blocks, general_body, (m, l, acc))[1:]

        def fixed_max_body(n, carry):
            l, acc = carry
            ja = qi - 2 - 2 * n
            jb = ja - 1
            b_ok = jnp.logical_and(jb >= 0, 2 * n + 1 < n_blocks)
            jb = jnp.maximum(jb, 0)
            p_a = jnp.exp2(logits_t(q2, ja, c0) - m)
            p_b = jnp.exp2(logits_t(q2, jb, c0 + jnp.where(b_ok, 0.0, MASKED_LOGIT)) - m)
            l = l + jnp.sum(p_a, axis=0, keepdims=True) + jnp.sum(p_b, axis=0, keepdims=True)
            acc = acc + weighted_values_t(p_a, ja) + weighted_values_t(p_b, jb)
            return l, acc

        def fixed_max_loop(_):
            return lax.fori_loop(0, (n_blocks + 1) // 2, fixed_max_body, (l, acc))

        max_is_final = excess + forget_gap(qi - 2) <= 0.0
        l, acc = lax.cond(max_is_final, fixed_max_loop, general_loop, None)
        return (acc / l).T

    near = [near_pass(t) for t in range(tiles)]
    for t, state in enumerate(near):
        o_ref[0, t * blk:(t + 1) * blk, :] = far_pass(*state).astype(o_ref.dtype)


def _fox_attention(qkv, c, blk, tiles):
    _, batch, n_heads, seq, _ = qkv.shape
    nb = seq // blk
    rows = tiles * blk
    cflat = c.reshape(batch * n_heads, nb, blk)
    cstart = cflat[:, :, 0]
    cend = cflat[:, :, blk - 1]
    cblocks = cflat.reshape(batch * n_heads, nb, 1, blk)
    kernel = functools.partial(_fox_kernel, n_heads=n_heads, blk=blk, scale=HEAD_DIM ** -0.5)
    grid_spec = pltpu.PrefetchScalarGridSpec(
        num_scalar_prefetch=2,
        grid=(batch, n_heads, seq // rows),
        in_specs=[
            pl.BlockSpec((1, 1, 1, rows, HEAD_DIM), lambda b, h, i, *_: (0, b, h, i, 0)),
            pl.BlockSpec((1, 1, 1, seq, HEAD_DIM), lambda b, h, i, *_: (1, b, h, 0, 0)),
            pl.BlockSpec((1, 1, 1, seq, HEAD_DIM), lambda b, h, i, *_: (2, b, h, 0, 0)),
            pl.BlockSpec((1, nb, 1, blk), lambda b, h, i, *_: (b * n_heads + h, 0, 0, 0)),
        ],
        out_specs=pl.BlockSpec((1, rows, HEAD_DIM), lambda b, h, i, *_: (b, i, h)),
        scratch_shapes=[pltpu.SMEM((1,), F32)],
    )
    return pl.pallas_call(
        kernel,
        grid_spec=grid_spec,
        out_shape=jax.ShapeDtypeStruct((batch, seq, n_heads * HEAD_DIM), BF16),
        compiler_params=_params(("parallel", "parallel", "arbitrary")),
        name="fox_attention",
    )(cstart, cend, qkv, qkv, qkv, cblocks)


def _sb_kernel(q_ref, k_ref, v_ref, o_ref, *, sub, scale):
    qi = pl.program_id(2)
    n_sub = q_ref.shape[3] // sub
    row = lax.broadcasted_iota(jnp.int32, (sub, sub), 0)
    col = lax.broadcasted_iota(jnp.int32, (sub, sub), 1)
    later = (row > col).astype(BF16)
    strict = col < row

    def block(y_q, j, carry, acc, masked, y_bias=None):
        kblk = k_ref[0, 0, 0, pl.ds(j * sub, sub), :]
        vblk = v_ref[0, 0, 0, pl.ds(j * sub, sub), :]
        y = lax.dot_general(y_q, kblk, (((1,), (1,)), ((), ())), preferred_element_type=F32)
        if y_bias is not None:
            y = y + y_bias
        if masked:
            y = jnp.where(strict, y, -MASKED_LOGIT)
        log2_not_beta = jnp.minimum(y, 0.0) - jnp.log2(1.0 + jnp.exp2(-jnp.abs(y)))
        after = jnp.dot(log2_not_beta.astype(BF16), later, preferred_element_type=F32) + carry
        a = jnp.exp2(log2_not_beta - y + after)
        acc = acc + jnp.dot(a.astype(BF16), vblk, preferred_element_type=F32)
        carry = carry + jnp.sum(log2_not_beta, axis=-1, keepdims=True)
        return carry, acc

    states = []
    for t in range(n_sub):
        y_q = (q_ref[0, 0, 0, t * sub:(t + 1) * sub, :].astype(F32)
               * (-scale * LOG2_E)).astype(BF16)
        jd = qi * n_sub + t
        carry, acc = block(y_q, jd, jnp.zeros((sub, 1), F32), jnp.zeros((sub, HEAD_DIM), F32),
                           True)
        if t == 0:
            carry, acc = block(y_q, jnp.maximum(jd - 1, 0), carry, acc, False,
                               y_bias=jnp.where(jd >= 1, 0.0, -MASKED_LOGIT))
        else:
            carry, acc = block(y_q, jd - 1, carry, acc, False)
        states.append((y_q, jd, carry, acc))

    def walk_back(_):
        accs = []
        for y_q, jd, carry, acc in states:
            def cond(state):
                j, carry, _ = state
                return jnp.logical_and(j >= 0, jnp.max(carry) >= EXP_IS_ZERO_BELOW * LOG2_E)

            def body(state, y_q=y_q):
                j, carry, acc = state
                carry, acc = block(y_q, j, carry, acc, False)
                return j - 1, carry, acc

            accs.append(lax.while_loop(cond, body, (jd - 2, carry, acc))[2])
        return accs

    highest_carry = jnp.max(functools.reduce(jnp.maximum, [s[2] for s in states]))
    accs = lax.cond(highest_carry >= EXP_IS_ZERO_BELOW * LOG2_E, walk_back,
                    lambda _: [s[3] for s in states], None)
    for t, acc in enumerate(accs):
        o_ref[0, t * sub:(t + 1) * sub, :] = acc.astype(o_ref.dtype)


def _sb_attention(qkv, blk, sub):
    _, batch, n_heads, seq, _ = qkv.shape
    nb = seq // blk
    kernel = functools.partial(_sb_kernel, sub=sub, scale=HEAD_DIM ** -0.5)
    return pl.pallas_call(
        kernel,
        grid=(batch, n_heads, nb),
        in_specs=[
            pl.BlockSpec((1, 1, 1, blk, HEAD_DIM), lambda b, h, i: (3, b, h, i, 0)),
            pl.BlockSpec((1, 1, 1, seq, HEAD_DIM), lambda b, h, i: (4, b, h, 0, 0)),
            pl.BlockSpec((1, 1, 1, seq, HEAD_DIM), lambda b, h, i: (5, b, h, 0, 0)),
        ],
        out_specs=pl.BlockSpec((1, blk, HEAD_DIM), lambda b, h, i: (b, i, h)),
        out_shape=jax.ShapeDtypeStruct((batch, seq, n_heads * HEAD_DIM), BF16),
        compiler_params=_params(("parallel", "parallel", "arbitrary")),
        name="sb_attention",
    )(qkv, qkv, qkv)


def _merge_kernel(ya_ref, yb_ref, ga_ref, gb_ref, x_ref, wa_ref, wb_ref, wo_ref, g_ref,
                  x2_ref, h_ref):
    ya = jnp.dot(ya_ref[...], wa_ref[...], preferred_element_type=F32)
    yb = jnp.dot(yb_ref[...], wb_ref[...], preferred_element_type=F32)
    merged = ga_ref[...].astype(F32) * ya + gb_ref[...].astype(F32) * yb
    x2 = x_ref[...] + jnp.dot(merged.astype(BF16), wo_ref[...], preferred_element_type=F32)
    x2_ref[...] = x2
    h_ref[...] = _rms_scale(x2, g_ref[...]).astype(h_ref.dtype)


def _merge_project(ya, yb, gates, x2d, wa, wb, wo, g, tm):
    t, d = x2d.shape
    wa_w = ya.shape[1]
    wb_w = yb.shape[1]
    resident = lambda shape: pl.BlockSpec(shape, lambda i: (0, 0), pipeline_mode=pl.Buffered(1))
    return pl.pallas_call(
        _merge_kernel,
        grid=(t // tm,),
        in_specs=[
            pl.BlockSpec((tm, wa_w), lambda i: (i, 0)),
            pl.BlockSpec((tm, wb_w), lambda i: (i, 0)),
            pl.BlockSpec((tm, d), lambda i: (i, 0)),
            pl.BlockSpec((tm, d), lambda i: (i, 1)),
            pl.BlockSpec((tm, d), lambda i: (i, 0)),
            resident((wa_w, d)),
            resident((wb_w, d)),
            resident((d, d)),
            resident((1, d)),
        ],
        out_specs=[
            pl.BlockSpec((tm, d), lambda i: (i, 0)),
            pl.BlockSpec((tm, d), lambda i: (i, 0)),
        ],
        out_shape=[
            jax.ShapeDtypeStruct((t, d), F32),
            jax.ShapeDtypeStruct((t, d), BF16),
        ],
        compiler_params=_params(("parallel",)),
        name="merge_project",
    )(ya, yb, gates, gates, x2d, wa, wb, wo, g)


def _mlp_kernel(h_ref, wu_ref, wd_ref, x2_ref, g_ref, o_ref, acc_ref, *, final_norm):
    j = pl.program_id(1)

    @pl.when(j == 0)
    def _():
        acc_ref[...] = x2_ref[...]

    u = jnp.maximum(jnp.dot(h_ref[...], wu_ref[...], preferred_element_type=F32), 0.0)
    acc_ref[...] += jnp.dot((u * u).astype(BF16), wd_ref[...], preferred_element_type=F32)

    @pl.when(j == pl.num_programs(1) - 1)
    def _():
        out = acc_ref[...]
        o_ref[...] = (_rms_scale(out, g_ref[...]) if final_norm else out).astype(o_ref.dtype)


def _mlp(h, wu, wd, x2, g, tm, tf, final_norm):
    t, d = x2.shape
    f = wu.shape[1]
    return pl.pallas_call(
        functools.partial(_mlp_kernel, final_norm=final_norm),
        grid=(t // tm, f // tf),
        in_specs=[
            pl.BlockSpec((tm, d), lambda i, j: (i, 0)),
            pl.BlockSpec((d, tf), lambda i, j: (0, j)),
            pl.BlockSpec((tf, d), lambda i, j: (j, 0)),
            pl.BlockSpec((tm, d), lambda i, j: (i, 0)),
            pl.BlockSpec((1, d), lambda i, j: (0, 0)),
        ],
        out_specs=pl.BlockSpec((tm, d), lambda i, j: (i, 0)),
        out_shape=jax.ShapeDtypeStruct((t, d), F32),
        scratch_shapes=[pltpu.VMEM((tm, d), F32)],
        compiler_params=_params(("parallel", "arbitrary")),
        name="mlp",
    )(h, wu, wd, x2, g)


def _tile(n, want):
    t = min(n, want)
    while n % t:
        t //= 2
    return t


def kernel(x, norm_mix_g, w_in, b_forget, w_out_fox, w_out_sb, w_out, norm_mlp_g, w_mlp_up,
           w_mlp_down, norm_final_g):
    batch, seq, d = x.shape
    depth = w_in.shape[0]
    n_heads_fox = b_forget.shape[-1]
    width_fox = w_out_fox.shape[1]
    width_sb = w_out_sb.shape[1]
    n_heads_sb = width_sb // HEAD_DIM
    assert width_fox == n_heads_fox * HEAD_DIM and n_heads_fox == n_heads_sb
    assert n_heads_fox <= 8 and seq % 1024 == 0 and d % V7X_LANES == 0
    t = batch * seq
    x2d = x.reshape(t, d)

    for l in range(depth):
        w = w_in[l].astype(BF16)
        o_f = 3 * width_fox
        o_sb = o_f + n_heads_fox
        o_g = o_sb + 3 * width_sb
        w_qkv_fox = w[:, :o_f]
        w_qkv_sb = w[:, o_sb:o_g]
        w_f = jnp.pad(w[:, o_f:o_sb], ((0, 0), (0, V7X_LANES - n_heads_fox)))
        b_f = jnp.pad(b_forget[l], (0, V7X_LANES - n_heads_fox)).reshape(1, V7X_LANES)
        w_g = w[:, o_g:]

        xn, lf = _norm_forget(x2d, norm_mix_g[l].reshape(1, d), w_f, b_f, _tile(t, 512))
        c = _forget_cumsum(lf.reshape(batch, seq, V7X_LANES), n_heads_fox, _tile(seq, 512))
        qkv = _proj_heads(xn, w_qkv_fox, w_qkv_sb, batch, n_heads_fox, _tile(seq, 1024))
        gates = _proj_gates(xn, w_g, _tile(t, 1024), _tile(2 * d, 1024))

        ya = _fox_attention(qkv, c, 512, 2).reshape(t, width_fox)
        yb = _sb_attention(qkv, 1024, 256).reshape(t, width_sb)

        x2d, h = _merge_project(ya, yb, gates, x2d, w_out_fox[l].astype(BF16),
                                w_out_sb[l].astype(BF16), w_out[l].astype(BF16),
                                norm_mlp_g[l].reshape(1, d), _tile(t, 512))
        x2d = _mlp(h, w_mlp_up[l].astype(BF16), w_mlp_down[l].astype(BF16), x2d,
                   norm_final_g.reshape(1, d), _tile(t, 512), _tile(w_mlp_up.shape[2], 1024),
                   final_norm=(l == depth - 1))
    return x2d.reshape(batch, seq, d)
```

```python
import functools

import jax
import jax.numpy as jnp
from jax import lax
from jax.experimental import pallas as pl
from jax.experimental.pallas import tpu as pltpu

HEAD_DIM = 128
RMS_EPS = 1e-6
MASKED_LOGIT = -1e30
EXP_IS_ZERO_BELOW = -104.0
LOG2_E = 1.4426950408889634
V7X_LANES = 128
V7X_VMEM_LIMIT_BYTES = 60 * 1024 * 1024

F32 = jnp.float32
BF16 = jnp.bfloat16


def _params(semantics, vmem_bytes=V7X_VMEM_LIMIT_BYTES):
    return pltpu.CompilerParams(dimension_semantics=semantics, vmem_limit_bytes=vmem_bytes)


def _log_sigmoid(u):
    return jnp.minimum(u, 0.0) - jnp.log1p(jnp.exp(-jnp.abs(u)))


def _rms_scale(x, g):
    ms = jnp.mean(x * x, axis=-1, keepdims=True)
    return x * lax.rsqrt(ms + RMS_EPS) * g


def _norm_forget_kernel(x_ref, g_ref, wf_ref, bf_ref, xn_ref, lf_ref):
    xn = _rms_scale(x_ref[...], g_ref[...]).astype(BF16)
    xn_ref[...] = xn
    f = jnp.dot(xn, wf_ref[...], preferred_element_type=F32) + bf_ref[...]
    lf_ref[...] = _log_sigmoid(f)


def _norm_forget(x2d, g, wf, bf, tm):
    t, d = x2d.shape
    return pl.pallas_call(
        _norm_forget_kernel,
        grid=(t // tm,),
        in_specs=[
            pl.BlockSpec((tm, d), lambda i: (i, 0)),
            pl.BlockSpec((1, d), lambda i: (0, 0)),
            pl.BlockSpec((d, V7X_LANES), lambda i: (0, 0)),
            pl.BlockSpec((1, V7X_LANES), lambda i: (0, 0)),
        ],
        out_specs=[
            pl.BlockSpec((tm, d), lambda i: (i, 0)),
            pl.BlockSpec((tm, V7X_LANES), lambda i: (i, 0)),
        ],
        out_shape=[
            jax.ShapeDtypeStruct((t, d), BF16),
            jax.ShapeDtypeStruct((t, V7X_LANES), F32),
        ],
        compiler_params=_params(("parallel",)),
        name="norm_forget",
    )(x2d, g, wf, bf)


def _cumsum_kernel(lf_ref, c_ref, carry_ref, *, n_heads):
    @pl.when(pl.program_id(1) == 0)
    def _():
        carry_ref[...] = jnp.zeros_like(carry_ref)

    tc = lf_ref.shape[1]
    lft = lf_ref[0].T[:n_heads, :]
    row = lax.broadcasted_iota(jnp.int32, (tc, tc), 0)
    col = lax.broadcasted_iota(jnp.int32, (tc, tc), 1)
    upper = (row <= col).astype(BF16)
    cs = carry_ref[:, :1]
    rest = lft
    for _ in range(3):
        piece = rest.astype(BF16)
        cs = cs + jnp.dot(piece, upper, preferred_element_type=F32)
        rest = rest - piece.astype(F32)
    c_ref[0] = cs
    carry_ref[...] = jnp.broadcast_to(cs[:, tc - 1:tc], carry_ref.shape)


def _forget_cumsum(lf, n_heads, tc):
    b, s, _ = lf.shape
    return pl.pallas_call(
        functools.partial(_cumsum_kernel, n_heads=n_heads),
        grid=(b, s // tc),
        in_specs=[pl.BlockSpec((1, tc, V7X_LANES), lambda i, j: (i, j, 0))],
        out_specs=pl.BlockSpec((1, n_heads, tc), lambda i, j: (i, 0, j)),
        out_shape=jax.ShapeDtypeStruct((b, n_heads, s), F32),
        scratch_shapes=[pltpu.VMEM((n_heads, V7X_LANES), F32)],
        compiler_params=_params(("arbitrary", "arbitrary")),
        name="forget_cumsum",
    )(lf)


def _proj_heads_kernel(x_ref, wa_ref, wb_ref, o_ref, *, groups_a):
    j = pl.program_id(1)

    def emit(w_ref):
        acc = jnp.dot(x_ref[...], w_ref[...], preferred_element_type=F32)
        for hh in range(o_ref.shape[2]):
            o_ref[0, 0, hh] = acc[:, hh * HEAD_DIM:(hh + 1) * HEAD_DIM].astype(o_ref.dtype)

    @pl.when(j < groups_a)
    def _():
        emit(wa_ref)

    @pl.when(j >= groups_a)
    def _():
        emit(wb_ref)


def _proj_heads(xn, wa, wb, batch, n_heads, tm):
    t, d = xn.shape
    tn = n_heads * HEAD_DIM
    ga, gb = wa.shape[1] // tn, wb.shape[1] // tn
    s = t // batch
    nst = s // tm
    return pl.pallas_call(
        functools.partial(_proj_heads_kernel, groups_a=ga),
        grid=(t // tm, ga + gb),
        in_specs=[
            pl.BlockSpec((tm, d), lambda i, j: (i, 0)),
            pl.BlockSpec((d, tn), lambda i, j: (0, jnp.minimum(j, ga - 1))),
            pl.BlockSpec((d, tn), lambda i, j: (0, jnp.maximum(j - ga, 0))),
        ],
        out_specs=pl.BlockSpec((1, 1, n_heads, tm, HEAD_DIM),
                               lambda i, j: (j, i // nst, 0, i % nst, 0)),
        out_shape=jax.ShapeDtypeStruct((ga + gb, batch, n_heads, s, HEAD_DIM), BF16),
        compiler_params=_params(("parallel", "arbitrary")),
        name="proj_heads",
    )(xn, wa, wb)


def _proj_gate_kernel(x_ref, w_ref, o_ref):
    acc = jnp.dot(x_ref[...], w_ref[...], preferred_element_type=F32)
    o_ref[...] = (0.5 * jnp.tanh(0.5 * acc) + 0.5).astype(o_ref.dtype)


def _proj_gates(xn, w, tm, tn):
    t, d = xn.shape
    n = w.shape[1]
    return pl.pallas_call(
        _proj_gate_kernel,
        grid=(t // tm, n // tn),
        in_specs=[
            pl.BlockSpec((tm, d), lambda i, j: (i, 0)),
            pl.BlockSpec((d, tn), lambda i, j: (0, j)),
        ],
        out_specs=pl.BlockSpec((tm, tn), lambda i, j: (i, j)),
        out_shape=jax.ShapeDtypeStruct((t, n), BF16),
        compiler_params=_params(("parallel", "arbitrary")),
        name="proj_gates",
    )(xn, w)


def _fox_kernel(cstart_ref, cend_ref, q_ref, k_ref, v_ref, c_ref, o_ref, kn2_ref, *,
                n_heads, blk, scale):
    b, h, step = pl.program_id(0), pl.program_id(1), pl.program_id(2)
    bh = b * n_heads + h
    tiles = q_ref.shape[3] // blk
    key = lax.broadcasted_iota(jnp.int32, (blk, blk), 0)
    query = lax.broadcasted_iota(jnp.int32, (blk, blk), 1)
    ones_rows = jnp.ones((8, HEAD_DIM), BF16)

    @pl.when(step == 0)
    def _():
        kn2_ref[0] = jnp.float32(0.0)

    kf = k_ref[0, 0, 0, pl.ds(step * (tiles * blk), tiles * blk), :].astype(F32)
    kn2_ref[0] = jnp.maximum(kn2_ref[0], jnp.max(jnp.sum(kf * kf, axis=-1)))
    kn2 = kn2_ref[0]

    def logits_t(q2, j, shift):
        kblk = k_ref[0, 0, 0, pl.ds(j * blk, blk), :]
        s = lax.dot_general(kblk, q2, (((1,), (1,)), ((), ())), preferred_element_type=F32)
        bias_row = (shift - c_ref[0, j]) * LOG2_E
        bias_col = jnp.broadcast_to(bias_row, (V7X_LANES, blk)).T
        return s + jnp.concatenate([bias_col] * (blk // V7X_LANES), axis=1)

    def weighted_values_t(p_t, j):
        vblk = v_ref[0, 0, 0, pl.ds(j * blk, blk), :]
        return lax.dot_general(vblk, p_t.astype(BF16), (((0,), (0,)), ((), ())),
                               preferred_element_type=F32)

    def near_pass(t):
        qi = step * tiles + t
        c0 = cstart_ref[bh, qi]
        q2 = (q_ref[0, 0, 0, t * blk:(t + 1) * blk, :].astype(F32)
              * (scale * LOG2_E)).astype(BF16)
        j_prev = jnp.maximum(qi - 1, 0)
        s_prev = logits_t(q2, j_prev, c0 + jnp.where(qi >= 1, 0.0, MASKED_LOGIT))
        s_diag = jnp.where(key <= query, logits_t(q2, qi, c0), MASKED_LOGIT)
        m = jnp.maximum(jnp.max(s_prev, axis=0, keepdims=True),
                        jnp.max(s_diag, axis=0, keepdims=True))
        p_prev = jnp.exp2(s_prev - m)
        p_diag = jnp.exp2(s_diag - m)
        l = jnp.sum(p_prev, axis=0, keepdims=True) + jnp.sum(p_diag, axis=0, keepdims=True)
        acc = weighted_values_t(p_prev, j_prev) + weighted_values_t(p_diag, qi)
        q2f = q2.astype(F32)
        qn2 = lax.dot_general(ones_rows, (q2f * q2f).astype(BF16), (((1,), (1,)), ((), ())),
                              preferred_element_type=F32)[:1]
        excess = jnp.max(jnp.sqrt(qn2 * (1.01 * kn2)) - m)
        return qi, c0, q2, m, l, acc, excess

    def far_pass(qi, c0, q2, m, l, acc, excess):
        def forget_gap(j):
            return (c0 - cend_ref[bh, jnp.maximum(j, 0)]) * LOG2_E

        def count_cond(n):
            j = qi - 2 - n
            return jnp.logical_and(j >= 0,
                                   excess + forget_gap(j) >= EXP_IS_ZERO_BELOW * LOG2_E)

        n_blocks = lax.while_loop(count_cond, lambda n: n + 1, jnp.int32(0))

        def general_body(n, carry):
            m, l, acc = carry
            j = qi - 2 - n
            s = logits_t(q2, j, c0)
            m_new = jnp.maximum(m, jnp.max(s, axis=0, keepdims=True))
            p = jnp.exp2(s - m_new)
            alpha = jnp.exp2(m - m_new)
            l = alpha * l + jnp.sum(p, axis=0, keepdims=True)
            acc = alpha * acc + weighted_values_t(p, j)
            return m_new, l, acc

        def general_loop(_):
            return lax.fori_loop(0, n_blocks, general_body, (m, l, acc))[1:]

        def fixed_max_body(n, carry):
            l, acc = carry
            ja = qi - 2 - 2 * n
            jb = ja - 1
            b_ok = jnp.logical_and(jb >= 0, 2 * n + 1 < n_blocks)
            jb = jnp.maximum(jb, 0)
            p_a = jnp.exp2(logits_t(q2, ja, c0) - m)
            p_b = jnp.exp2(logits_t(q2, jb, c0 + jnp.where(b_ok, 0.0, MASKED_LOGIT)) - m)
            l = l + jnp.sum(p_a, axis=0, keepdims=True) + jnp.sum(p_b, axis=0, keepdims=True)
            acc = acc + weighted_values_t(p_a, ja) + weighted_values_t(p_b, jb)
            return l, acc

        def fixed_max_loop(_):
            return lax.fori_loop(0, (n_blocks + 1) // 2, fixed_max_body, (l, acc))

        max_is_final = excess + forget_gap(qi - 2) <= 0.0
        l, acc = lax.cond(max_is_final, fixed_max_loop, general_loop, None)
        return (acc / l).T

    near = [near_pass(t) for t in range(tiles)]
    for t, state in enumerate(near):
        o_ref[0, t * blk:(t + 1) * blk, :] = far_pass(*state).astype(o_ref.dtype)


def _fox_attention(qkv, c, blk, tiles):
    _, batch, n_heads, seq, _ = qkv.shape
    nb = seq // blk
    rows = tiles * blk
    cflat = c.reshape(batch * n_heads, nb, blk)
    cstart = cflat[:, :, 0]
    cend = cflat[:, :, blk - 1]
    cblocks = cflat.reshape(batch * n_heads, nb, 1, blk)
    kernel = functools.partial(_fox_kernel, n_heads=n_heads, blk=blk, scale=HEAD_DIM ** -0.5)
    grid_spec = pltpu.PrefetchScalarGridSpec(
        num_scalar_prefetch=2,
        grid=(batch, n_heads, seq // rows),
        in_specs=[
            pl.BlockSpec((1, 1, 1, rows, HEAD_DIM), lambda b, h, i, *_: (0, b, h, i, 0)),
            pl.BlockSpec((1, 1, 1, seq, HEAD_DIM), lambda b, h, i, *_: (1, b, h, 0, 0)),
            pl.BlockSpec((1, 1, 1, seq, HEAD_DIM), lambda b, h, i, *_: (2, b, h, 0, 0)),
            pl.BlockSpec((1, nb, 1, blk), lambda b, h, i, *_: (b * n_heads + h, 0, 0, 0)),
        ],
        out_specs=pl.BlockSpec((1, rows, HEAD_DIM), lambda b, h, i, *_: (b, i, h)),
        scratch_shapes=[pltpu.SMEM((1,), F32)],
    )
    return pl.pallas_call(
        kernel,
        grid_spec=grid_spec,
        out_shape=jax.ShapeDtypeStruct((batch, seq, n_heads * HEAD_DIM), BF16),
        compiler_params=_params(("parallel", "parallel", "arbitrary")),
        name="fox_attention",
    )(cstart, cend, qkv, qkv, qkv, cblocks)


def _sb_kernel(q_ref, k_ref, v_ref, o_ref, *, sub, scale):
    qi = pl.program_id(2)
    n_sub = q_ref.shape[3] // sub
    row = lax.broadcasted_iota(jnp.int32, (sub, sub), 0)
    col = lax.broadcasted_iota(jnp.int32, (sub, sub), 1)
    later = (row > col).astype(BF16)
    strict = col < row

    def block(y_q, j, carry, acc, masked, y_bias=None):
        kblk = k_ref[0, 0, 0, pl.ds(j * sub, sub), :]
        vblk = v_ref[0, 0, 0, pl.ds(j * sub, sub), :]
        y = lax.dot_general(y_q, kblk, (((1,), (1,)), ((), ())), preferred_element_type=F32)
        if y_bias is not None:
            y = y + y_bias
        if masked:
            y = jnp.where(strict, y, -MASKED_LOGIT)
        log2_not_beta = jnp.minimum(y, 0.0) - jnp.log2(1.0 + jnp.exp2(-jnp.abs(y)))
        after = jnp.dot(log2_not_beta.astype(BF16), later, preferred_element_type=F32) + carry
        a = jnp.exp2(log2_not_beta - y + after)
        acc = acc + jnp.dot(a.astype(BF16), vblk, preferred_element_type=F32)
        carry = carry + jnp.sum(log2_not_beta, axis=-1, keepdims=True)
        return carry, acc

    states = []
    for t in range(n_sub):
        y_q = (q_ref[0, 0, 0, t * sub:(t + 1) * sub, :].astype(F32)
               * (-scale * LOG2_E)).astype(BF16)
        jd = qi * n_sub + t
        carry, acc = block(y_q, jd, jnp.zeros((sub, 1), F32), jnp.zeros((sub, HEAD_DIM), F32),
                           True)
        if t == 0:
            carry, acc = block(y_q, jnp.maximum(jd - 1, 0), carry, acc, False,
                               y_bias=jnp.where(jd >= 1, 0.0, -MASKED_LOGIT))
        else:
            carry, acc = block(y_q, jd - 1, carry, acc, False)
        states.append((y_q, jd, carry, acc))

    def walk_back(_):
        accs = []
        for y_q, jd, carry, acc in states:
            def cond(state):
                j, carry, _ = state
                return jnp.logical_and(j >= 0, jnp.max(carry) >= EXP_IS_ZERO_BELOW * LOG2_E)

            def body(state, y_q=y_q):
                j, carry, acc = state
                carry, acc = block(y_q, j, carry, acc, False)
                return j - 1, carry, acc

            accs.append(lax.while_loop(cond, body, (jd - 2, carry, acc))[2])
        return accs

    highest_carry = jnp.max(functools.reduce(jnp.maximum, [s[2] for s in states]))
    accs = lax.cond(highest_carry >= EXP_IS_ZERO_BELOW * LOG2_E, walk_back,
                    lambda _: [s[3] for s in states], None)
    for t, acc in enumerate(accs):
        o_ref[0, t * sub:(t + 1) * sub, :] = acc.astype(o_ref.dtype)


def _sb_attention(qkv, blk, sub):
    _, batch, n_heads, seq, _ = qkv.shape
    nb = seq // blk
    kernel = functools.partial(_sb_kernel, sub=sub, scale=HEAD_DIM ** -0.5)
    return pl.pallas_call(
        kernel,
        grid=(batch, n_heads, nb),
        in_specs=[
            pl.BlockSpec((1, 1, 1, blk, HEAD_DIM), lambda b, h, i: (3, b, h, i, 0)),
            pl.BlockSpec((1, 1, 1, seq, HEAD_DIM), lambda b, h, i: (4, b, h, 0, 0)),
            pl.BlockSpec((1, 1, 1, seq, HEAD_DIM), lambda b, h, i: (5, b, h, 0, 0)),
        ],
        out_specs=pl.BlockSpec((1, blk, HEAD_DIM), lambda b, h, i: (b, i, h)),
        out_shape=jax.ShapeDtypeStruct((batch, seq, n_heads * HEAD_DIM), BF16),
        compiler_params=_params(("parallel", "parallel", "arbitrary")),
        name="sb_attention",
    )(qkv, qkv, qkv)


def _merge_kernel(ya_ref, yb_ref, ga_ref, gb_ref, x_ref, wa_ref, wb_ref, wo_ref, g_ref,
                  x2_ref, h_ref):
    ya = jnp.dot(ya_ref[...], wa_ref[...], preferred_element_type=F32)
    yb = jnp.dot(yb_ref[...], wb_ref[...], preferred_element_type=F32)
    merged = ga_ref[...].astype(F32) * ya + gb_ref[...].astype(F32) * yb
    x2 = x_ref[...] + jnp.dot(merged.astype(BF16), wo_ref[...], preferred_element_type=F32)
    x2_ref[...] = x2
    h_ref[...] = _rms_scale(x2, g_ref[...]).astype(h_ref.dtype)


def _merge_project(ya, yb, gates, x2d, wa, wb, wo, g, tm):
    t, d = x2d.shape
    wa_w = ya.shape[1]
    wb_w = yb.shape[1]
    resident = lambda shape: pl.BlockSpec(shape, lambda i: (0, 0), pipeline_mode=pl.Buffered(1))
    return pl.pallas_call(
        _merge_kernel,
        grid=(t // tm,),
        in_specs=[
            pl.BlockSpec((tm, wa_w), lambda i: (i, 0)),
            pl.BlockSpec((tm, wb_w), lambda i: (i, 0)),
            pl.BlockSpec((tm, d), lambda i: (i, 0)),
            pl.BlockSpec((tm, d), lambda i: (i, 1)),
            pl.BlockSpec((tm, d), lambda i: (i, 0)),
            resident((wa_w, d)),
            resident((wb_w, d)),
            resident((d, d)),
            resident((1, d)),
        ],
        out_specs=[
            pl.BlockSpec((tm, d), lambda i: (i, 0)),
            pl.BlockSpec((tm, d), lambda i: (i, 0)),
        ],
        out_shape=[
            jax.ShapeDtypeStruct((t, d), F32),
            jax.ShapeDtypeStruct((t, d), BF16),
        ],
        compiler_params=_params(("parallel",)),
        name="merge_project",
    )(ya, yb, gates, gates, x2d, wa, wb, wo, g)


def _mlp_kernel(h_ref, wu_ref, wd_ref, x2_ref, g_ref, o_ref, acc_ref, *, final_norm):
    j = pl.program_id(1)

    @pl.when(j == 0)
    def _():
        acc_ref[...] = x2_ref[...]

    u = jnp.maximum(jnp.dot(h_ref[...], wu_ref[...], preferred_element_type=F32), 0.0)
    acc_ref[...] += jnp.dot((u * u).astype(BF16), wd_ref[...], preferred_element_type=F32)

    @pl.when(j == pl.num_programs(1) - 1)
    def _():
        out = acc_ref[...]
        o_ref[...] = (_rms_scale(out, g_ref[...]) if final_norm else out).astype(o_ref.dtype)


def _mlp(h, wu, wd, x2, g, tm, tf, final_norm):
    t, d = x2.shape
    f = wu.shape[1]
    return pl.pallas_call(
        functools.partial(_mlp_kernel, final_norm=final_norm),
        grid=(t // tm, f // tf),
        in_specs=[
            pl.BlockSpec((tm, d), lambda i, j: (i, 0)),
            pl.BlockSpec((d, tf), lambda i, j: (0, j)),
            pl.BlockSpec((tf, d), lambda i, j: (j, 0)),
            pl.BlockSpec((tm, d), lambda i, j: (i, 0)),
            pl.BlockSpec((1, d), lambda i, j: (0, 0)),
        ],
        out_specs=pl.BlockSpec((tm, d), lambda i, j: (i, 0)),
        out_shape=jax.ShapeDtypeStruct((t, d), F32),
        scratch_shapes=[pltpu.VMEM((tm, d), F32)],
        compiler_params=_params(("parallel", "arbitrary")),
        name="mlp",
    )(h, wu, wd, x2, g)


def _tile(n, want):
    t = min(n, want)
    while n % t:
        t //= 2
    return t


def kernel(x, norm_mix_g, w_in, b_forget, w_out_fox, w_out_sb, w_out, norm_mlp_g, w_mlp_up,
           w_mlp_down, norm_final_g):
    batch, seq, d = x.shape
    depth = w_in.shape[0]
    n_heads_fox = b_forget.shape[-1]
    width_fox = w_out_fox.shape[1]
    width_sb = w_out_sb.shape[1]
    n_heads_sb = width_sb // HEAD_DIM
    assert width_fox == n_heads_fox * HEAD_DIM and n_heads_fox == n_heads_sb
    assert n_heads_fox <= 8 and seq % 2048 == 0 and d % V7X_LANES == 0
    t = batch * seq
    x2d = x.reshape(t, d)

    for l in range(depth):
        w = w_in[l].astype(BF16)
        o_f = 3 * width_fox
        o_sb = o_f + n_heads_fox
        o_g = o_sb + 3 * width_sb
        w_qkv_fox = w[:, :o_f]
        w_qkv_sb = w[:, o_sb:o_g]
        w_f = jnp.pad(w[:, o_f:o_sb], ((0, 0), (0, V7X_LANES - n_heads_fox)))
        b_f = jnp.pad(b_forget[l], (0, V7X_LANES - n_heads_fox)).reshape(1, V7X_LANES)
        w_g = w[:, o_g:]

        xn, lf = _norm_forget(x2d, norm_mix_g[l].reshape(1, d), w_f, b_f, _tile(t, 1024))
        c = _forget_cumsum(lf.reshape(batch, seq, V7X_LANES), n_heads_fox, _tile(seq, 512))
        qkv = _proj_heads(xn, w_qkv_fox, w_qkv_sb, batch, n_heads_fox, _tile(seq, 2048))
        gates = _proj_gates(xn, w_g, _tile(t, 2048), _tile(2 * d, 1024))

        ya = _fox_attention(qkv, c, 512, 4).reshape(t, width_fox)
        yb = _sb_attention(qkv, 2048, 256).reshape(t, width_sb)

        x2d, h = _merge_project(ya, yb, gates, x2d, w_out_fox[l].astype(BF16),
                                w_out_sb[l].astype(BF16), w_out[l].astype(BF16),
                                norm_mlp_g[l].reshape(1, d), _tile(t, 512))
        x2d = _mlp(h, w_mlp_up[l].astype(BF16), w_mlp_down[l].astype(BF16), x2d,
                   norm_final_g.reshape(1, d), _tile(t, 512), _tile(w_mlp_up.shape[2], 1024),
                   final_norm=(l == depth - 1))
    return x2d.reshape(batch, seq, d)
```

```python
import functools

import jax
import jax.numpy as jnp
from jax import lax
from jax.experimental import pallas as pl
from jax.experimental.pallas import tpu as pltpu

HEAD_DIM = 128
RMS_EPS = 1e-6
MASKED_LOGIT = -1e30
EXP_IS_ZERO_BELOW = -104.0
LOG2_E = 1.4426950408889634
V7X_LANES = 128
V7X_VMEM_LIMIT_BYTES = 60 * 1024 * 1024

F32 = jnp.float32
BF16 = jnp.bfloat16


def _params(semantics, vmem_bytes=V7X_VMEM_LIMIT_BYTES):
    return pltpu.CompilerParams(dimension_semantics=semantics, vmem_limit_bytes=vmem_bytes)


def _log_sigmoid(u):
    return jnp.minimum(u, 0.0) - jnp.log1p(jnp.exp(-jnp.abs(u)))


def _rms_scale(x, g):
    ms = jnp.mean(x * x, axis=-1, keepdims=True)
    return x * lax.rsqrt(ms + RMS_EPS) * g


def _norm_forget_kernel(x_ref, g_ref, wf_ref, bf_ref, xn_ref, lf_ref):
    xn = _rms_scale(x_ref[...], g_ref[...]).astype(BF16)
    xn_ref[...] = xn
    f = jnp.dot(xn, wf_ref[...], preferred_element_type=F32) + bf_ref[...]
    lf_ref[...] = _log_sigmoid(f)


def _norm_forget(x2d, g, wf, bf, tm):
    t, d = x2d.shape
    return pl.pallas_call(
        _norm_forget_kernel,
        grid=(t // tm,),
        in_specs=[
            pl.BlockSpec((tm, d), lambda i: (i, 0)),
            pl.BlockSpec((1, d), lambda i: (0, 0)),
            pl.BlockSpec((d, V7X_LANES), lambda i: (0, 0)),
            pl.BlockSpec((1, V7X_LANES), lambda i: (0, 0)),
        ],
        out_specs=[
            pl.BlockSpec((tm, d), lambda i: (i, 0)),
            pl.BlockSpec((tm, V7X_LANES), lambda i: (i, 0)),
        ],
        out_shape=[
            jax.ShapeDtypeStruct((t, d), BF16),
            jax.ShapeDtypeStruct((t, V7X_LANES), F32),
        ],
        compiler_params=_params(("parallel",)),
        name="norm_forget",
    )(x2d, g, wf, bf)


def _cumsum_kernel(lf_ref, c_ref, carry_ref, *, n_heads):
    @pl.when(pl.program_id(1) == 0)
    def _():
        carry_ref[...] = jnp.zeros_like(carry_ref)

    tc = lf_ref.shape[1]
    lft = lf_ref[0].T[:n_heads, :]
    row = lax.broadcasted_iota(jnp.int32, (tc, tc), 0)
    col = lax.broadcasted_iota(jnp.int32, (tc, tc), 1)
    upper = (row <= col).astype(BF16)
    cs = carry_ref[:, :1]
    rest = lft
    for _ in range(3):
        piece = rest.astype(BF16)
        cs = cs + jnp.dot(piece, upper, preferred_element_type=F32)
        rest = rest - piece.astype(F32)
    c_ref[0] = cs
    carry_ref[...] = jnp.broadcast_to(cs[:, tc - 1:tc], carry_ref.shape)


def _forget_cumsum(lf, n_heads, tc):
    b, s, _ = lf.shape
    return pl.pallas_call(
        functools.partial(_cumsum_kernel, n_heads=n_heads),
        grid=(b, s // tc),
        in_specs=[pl.BlockSpec((1, tc, V7X_LANES), lambda i, j: (i, j, 0))],
        out_specs=pl.BlockSpec((1, n_heads, tc), lambda i, j: (i, 0, j)),
        out_shape=jax.ShapeDtypeStruct((b, n_heads, s), F32),
        scratch_shapes=[pltpu.VMEM((n_heads, V7X_LANES), F32)],
        compiler_params=_params(("arbitrary", "arbitrary")),
        name="forget_cumsum",
    )(lf)


def _proj_heads_kernel(x_ref, wa_ref, wb_ref, o_ref, *, groups_a):
    j = pl.program_id(1)

    def emit(w_ref):
        acc = jnp.dot(x_ref[...], w_ref[...], preferred_element_type=F32)
        for hh in range(o_ref.shape[2]):
            o_ref[0, 0, hh] = acc[:, hh * HEAD_DIM:(hh + 1) * HEAD_DIM].astype(o_ref.dtype)

    @pl.when(j < groups_a)
    def _():
        emit(wa_ref)

    @pl.when(j >= groups_a)
    def _():
        emit(wb_ref)


def _proj_heads(xn, wa, wb, batch, n_heads, tm):
    t, d = xn.shape
    tn = n_heads * HEAD_DIM
    ga, gb = wa.shape[1] // tn, wb.shape[1] // tn
    s = t // batch
    nst = s // tm
    return pl.pallas_call(
        functools.partial(_proj_heads_kernel, groups_a=ga),
        grid=(t // tm, ga + gb),
        in_specs=[
            pl.BlockSpec((tm, d), lambda i, j: (i, 0)),
            pl.BlockSpec((d, tn), lambda i, j: (0, jnp.minimum(j, ga - 1))),
            pl.BlockSpec((d, tn), lambda i, j: (0, jnp.maximum(j - ga, 0))),
        ],
        out_specs=pl.BlockSpec((1, 1, n_heads, tm, HEAD_DIM),
                               lambda i, j: (j, i // nst, 0, i % nst, 0)),
        out_shape=jax.ShapeDtypeStruct((ga + gb, batch, n_heads, s, HEAD_DIM), BF16),
        compiler_params=_params(("parallel", "arbitrary")),
        name="proj_heads",
    )(xn, wa, wb)


def _proj_gate_kernel(x_ref, w_ref, o_ref):
    acc = jnp.dot(x_ref[...], w_ref[...], preferred_element_type=F32)
    o_ref[...] = (0.5 * jnp.tanh(0.5 * acc) + 0.5).astype(o_ref.dtype)


def _proj_gates(xn, w, tm, tn):
    t, d = xn.shape
    n = w.shape[1]
    return pl.pallas_call(
        _proj_gate_kernel,
        grid=(t // tm, n // tn),
        in_specs=[
            pl.BlockSpec((tm, d), lambda i, j: (i, 0)),
            pl.BlockSpec((d, tn), lambda i, j: (0, j)),
        ],
        out_specs=pl.BlockSpec((tm, tn), lambda i, j: (i, j)),
        out_shape=jax.ShapeDtypeStruct((t, n), BF16),
        compiler_params=_params(("parallel", "arbitrary")),
        name="proj_gates",
    )(xn, w)


def _fox_kernel(cstart_ref, cend_ref, q_ref, k_ref, v_ref, c_ref, o_ref, kn2_ref, *,
                n_heads, blk, scale):
    b, h, step = pl.program_id(0), pl.program_id(1), pl.program_id(2)
    bh = b * n_heads + h
    tiles = q_ref.shape[3] // blk
    key = lax.broadcasted_iota(jnp.int32, (blk, blk), 0)
    query = lax.broadcasted_iota(jnp.int32, (blk, blk), 1)
    ones_rows = jnp.ones((8, HEAD_DIM), BF16)

    @pl.when(step == 0)
    def _():
        kn2_ref[0] = jnp.float32(0.0)

    kf = k_ref[0, 0, 0, pl.ds(step * (tiles * blk), tiles * blk), :].astype(F32)
    kn2_ref[0] = jnp.maximum(kn2_ref[0], jnp.max(jnp.sum(kf * kf, axis=-1)))
    kn2 = kn2_ref[0]

    def logits_t(q2, j, shift):
        kblk = k_ref[0, 0, 0, pl.ds(j * blk, blk), :]
        s = lax.dot_general(kblk, q2, (((1,), (1,)), ((), ())), preferred_element_type=F32)
        bias_row = (shift - c_ref[0, j]) * LOG2_E
        bias_col = jnp.broadcast_to(bias_row, (V7X_LANES, blk)).T
        return s + jnp.concatenate([bias_col] * (blk // V7X_LANES), axis=1)

    def weighted_values_t(p_t, j):
        vblk = v_ref[0, 0, 0, pl.ds(j * blk, blk), :]
        return lax.dot_general(vblk, p_t.astype(BF16), (((0,), (0,)), ((), ())),
                               preferred_element_type=F32)

    def near_pass(t):
        qi = step * tiles + t
        c0 = cstart_ref[bh, qi]
        q2 = (q_ref[0, 0, 0, t * blk:(t + 1) * blk, :].astype(F32)
              * (scale * LOG2_E)).astype(BF16)
        j_prev = jnp.maximum(qi - 1, 0)
        s_prev = logits_t(q2, j_prev, c0 + jnp.where(qi >= 1, 0.0, MASKED_LOGIT))
        s_diag = jnp.where(key <= query, logits_t(q2, qi, c0), MASKED_LOGIT)
        m = jnp.maximum(jnp.max(s_prev, axis=0, keepdims=True),
                        jnp.max(s_diag, axis=0, keepdims=True))
        p_prev = jnp.exp2(s_prev - m)
        p_diag = jnp.exp2(s_diag - m)
        l = jnp.sum(p_prev, axis=0, keepdims=True) + jnp.sum(p_diag, axis=0, keepdims=True)
        acc = weighted_values_t(p_prev, j_prev) + weighted_values_t(p_diag, qi)
        q2f = q2.astype(F32)
        qn2 = lax.dot_general(ones_rows, (q2f * q2f).astype(BF16), (((1,), (1,)), ((), ())),
                              preferred_element_type=F32)[:1]
        excess = jnp.max(jnp.sqrt(qn2 * (1.01 * kn2)) - m)
        return qi, c0, q2, m, l, acc, excess

    def far_pass(qi, c0, q2, m, l, acc, excess):
        def forget_gap(j):
            return (c0 - cend_ref[bh, jnp.maximum(j, 0)]) * LOG2_E

        def count_cond(n):
            j = qi - 2 - n
            return jnp.logical_and(j >= 0,
                                   excess + forget_gap(j) >= EXP_IS_ZERO_BELOW * LOG2_E)

        n_blocks = lax.while_loop(count_cond, lambda n: n + 1, jnp.int32(0))

        def general_body(n, carry):
            m, l, acc = carry
            j = qi - 2 - n
            s = logits_t(q2, j, c0)
            m_new = jnp.maximum(m, jnp.max(s, axis=0, keepdims=True))
            p = jnp.exp2(s - m_new)
            alpha = jnp.exp2(m - m_new)
            l = alpha * l + jnp.sum(p, axis=0, keepdims=True)
            acc = alpha * acc + weighted_values_t(p, j)
            return m_new, l, acc

        def general_loop(_):
            return lax.fori_loop(0, n_blocks, general_body, (m, l, acc))[1:]

        def fixed_max_body(n, carry):
            l, acc = carry
            ja = qi - 2 - 2 * n
            jb = ja - 1
            b_ok = jnp.logical_and(jb >= 0, 2 * n + 1 < n_blocks)
            jb = jnp.maximum(jb, 0)
            p_a = jnp.exp2(logits_t(q2, ja, c0) - m)
            p_b = jnp.exp2(logits_t(q2, jb, c0 + jnp.where(b_ok, 0.0, MASKED_LOGIT)) - m)
            l = l + jnp.sum(p_a, axis=0, keepdims=True) + jnp.sum(p_b, axis=0, keepdims=True)
            acc = acc + weighted_values_t(p_a, ja) + weighted_values_t(p_b, jb)
            return l, acc

        def fixed_max_loop(_):
            return lax.fori_loop(0, (n_blocks + 1) // 2, fixed_max_body, (l, acc))

        max_is_final = excess + forget_gap(qi - 2) <= 0.0
        l, acc = lax.cond(max_is_final, fixed_max_loop, general_loop, None)
        return (acc / l).T

    near = [near_pass(t) for t in range(tiles)]
    for t, state in enumerate(near):
        o_ref[0, t * blk:(t + 1) * blk, :] = far_pass(*state).astype(o_ref.dtype)


def _fox_attention(qkv, c, blk, tiles):
    _, batch, n_heads, seq, _ = qkv.shape
    nb = seq // blk
    rows = tiles * blk
    cflat = c.reshape(batch * n_heads, nb, blk)
    cstart = cflat[:, :, 0]
    cend = cflat[:, :, blk - 1]
    cblocks = cflat.reshape(batch * n_heads, nb, 1, blk)
    kernel = functools.partial(_fox_kernel, n_heads=n_heads, blk=blk, scale=HEAD_DIM ** -0.5)
    grid_spec = pltpu.PrefetchScalarGridSpec(
        num_scalar_prefetch=2,
        grid=(batch, n_heads, seq // rows),
        in_specs=[
            pl.BlockSpec((1, 1, 1, rows, HEAD_DIM), lambda b, h, i, *_: (0, b, h, i, 0)),
            pl.BlockSpec((1, 1, 1, seq, HEAD_DIM), lambda b, h, i, *_: (1, b, h, 0, 0)),
            pl.BlockSpec((1, 1, 1, seq, HEAD_DIM), lambda b, h, i, *_: (2, b, h, 0, 0)),
            pl.BlockSpec((1, nb, 1, blk), lambda b, h, i, *_: (b * n_heads + h, 0, 0, 0)),
        ],
        out_specs=pl.BlockSpec((1, rows, HEAD_DIM), lambda b, h, i, *_: (b, i, h)),
        scratch_shapes=[pltpu.SMEM((1,), F32)],
    )
    return pl.pallas_call(
        kernel,
        grid_spec=grid_spec,
        out_shape=jax.ShapeDtypeStruct((batch, seq, n_heads * HEAD_DIM), BF16),
        compiler_params=_params(("parallel", "parallel", "arbitrary")),
        name="fox_attention",
    )(cstart, cend, qkv, qkv, qkv, cblocks)


def _sb_kernel(q_ref, k_ref, v_ref, o_ref, *, sub, scale):
    qi = pl.program_id(2)
    n_sub = q_ref.shape[3] // sub
    row = lax.broadcasted_iota(jnp.int32, (sub, sub), 0)
    col = lax.broadcasted_iota(jnp.int32, (sub, sub), 1)
    later = (row > col).astype(BF16)
    strict = col < row

    def block(y_q, j, carry, acc, masked, y_bias=None):
        kblk = k_ref[0, 0, 0, pl.ds(j * sub, sub), :]
        vblk = v_ref[0, 0, 0, pl.ds(j * sub, sub), :]
        y = lax.dot_general(y_q, kblk, (((1,), (1,)), ((), ())), preferred_element_type=F32)
        if y_bias is not None:
            y = y + y_bias
        if masked:
            y = jnp.where(strict, y, -MASKED_LOGIT)
        log2_not_beta = jnp.minimum(y, 0.0) - jnp.log2(1.0 + jnp.exp2(-jnp.abs(y)))
        after = jnp.dot(log2_not_beta.astype(BF16), later, preferred_element_type=F32) + carry
        a = jnp.exp2(log2_not_beta - y + after)
        acc = acc + jnp.dot(a.astype(BF16), vblk, preferred_element_type=F32)
        carry = carry + jnp.sum(log2_not_beta, axis=-1, keepdims=True)
        return carry, acc

    states = []
    for t in range(n_sub):
        y_q = (q_ref[0, 0, 0, t * sub:(t + 1) * sub, :].astype(F32)
               * (-scale * LOG2_E)).astype(BF16)
        jd = qi * n_sub + t
        carry, acc = block(y_q, jd, jnp.zeros((sub, 1), F32), jnp.zeros((sub, HEAD_DIM), F32),
                           True)
        if t == 0:
            carry, acc = block(y_q, jnp.maximum(jd - 1, 0), carry, acc, False,
                               y_bias=jnp.where(jd >= 1, 0.0, -MASKED_LOGIT))
        else:
            carry, acc = block(y_q, jd - 1, carry, acc, False)
        states.append((y_q, jd, carry, acc))

    def walk_back(_):
        accs = []
        for y_q, jd, carry, acc in states:
            def cond(state):
                j, carry, _ = state
                return jnp.logical_and(j >= 0, jnp.max(carry) >= EXP_IS_ZERO_BELOW * LOG2_E)

            def body(state, y_q=y_q):
                j, carry, acc = state
                carry, acc = block(y_q, j, carry, acc, False)
                return j - 1, carry, acc

            accs.append(lax.while_loop(cond, body, (jd - 2, carry, acc))[2])
        return accs

    highest_carry = jnp.max(functools.reduce(jnp.maximum, [s[2] for s in states]))
    accs = lax.cond(highest_carry >= EXP_IS_ZERO_BELOW * LOG2_E, walk_back,
                    lambda _: [s[3] for s in states], None)
    for t, acc in enumerate(accs):
        o_ref[0, t * sub:(t + 1) * sub, :] = acc.astype(o_ref.dtype)


def _sb_attention(qkv, blk, sub):
    _, batch, n_heads, seq, _ = qkv.shape
    nb = seq // blk
    kernel = functools.partial(_sb_kernel, sub=sub, scale=HEAD_DIM ** -0.5)
    return pl.pallas_call(
        kernel,
        grid=(batch, n_heads, nb),
        in_specs=[
            pl.BlockSpec((1, 1, 1, blk, HEAD_DIM), lambda b, h, i: (3, b, h, i, 0)),
            pl.BlockSpec((1, 1, 1, seq, HEAD_DIM), lambda b, h, i: (4, b, h, 0, 0)),
            pl.BlockSpec((1, 1, 1, seq, HEAD_DIM), lambda b, h, i: (5, b, h, 0, 0)),
        ],
        out_specs=pl.BlockSpec((1, blk, HEAD_DIM), lambda b, h, i: (b, i, h)),
        out_shape=jax.ShapeDtypeStruct((batch, seq, n_heads * HEAD_DIM), BF16),
        compiler_params=_params(("parallel", "parallel", "arbitrary")),
        name="sb_attention",
    )(qkv, qkv, qkv)


def _merge_kernel(ya_ref, yb_ref, ga_ref, gb_ref, x_ref, wa_ref, wb_ref, wo_ref, g_ref,
                  x2_ref, h_ref):
    ya = jnp.dot(ya_ref[...], wa_ref[...], preferred_element_type=F32)
    yb = jnp.dot(yb_ref[...], wb_ref[...], preferred_element_type=F32)
    merged = ga_ref[...].astype(F32) * ya + gb_ref[...].astype(F32) * yb
    x2 = x_ref[...] + jnp.dot(merged.astype(BF16), wo_ref[...], preferred_element_type=F32)
    x2_ref[...] = x2
    h_ref[...] = _rms_scale(x2, g_ref[...]).astype(h_ref.dtype)


def _merge_project(ya, yb, gates, x2d, wa, wb, wo, g, tm):
    t, d = x2d.shape
    wa_w = ya.shape[1]
    wb_w = yb.shape[1]
    resident = lambda shape: pl.BlockSpec(shape, lambda i: (0, 0), pipeline_mode=pl.Buffered(1))
    return pl.pallas_call(
        _merge_kernel,
        grid=(t // tm,),
        in_specs=[
            pl.BlockSpec((tm, wa_w), lambda i: (i, 0)),
            pl.BlockSpec((tm, wb_w), lambda i: (i, 0)),
            pl.BlockSpec((tm, d), lambda i: (i, 0)),
            pl.BlockSpec((tm, d), lambda i: (i, 1)),
            pl.BlockSpec((tm, d), lambda i: (i, 0)),
            resident((wa_w, d)),
            resident((wb_w, d)),
            resident((d, d)),
            resident((1, d)),
        ],
        out_specs=[
            pl.BlockSpec((tm, d), lambda i: (i, 0)),
            pl.BlockSpec((tm, d), lambda i: (i, 0)),
        ],
        out_shape=[
            jax.ShapeDtypeStruct((t, d), F32),
            jax.ShapeDtypeStruct((t, d), BF16),
        ],
        compiler_params=_params(("parallel",)),
        name="merge_project",
    )(ya, yb, gates, gates, x2d, wa, wb, wo, g)


def _mlp_kernel(h_ref, wu_ref, wd_ref, x2_ref, g_ref, o_ref, *, final_norm):
    j = pl.program_id(1)

    @pl.when(j == 0)
    def _():
        o_ref[...] = x2_ref[...]

    u = jnp.maximum(jnp.dot(h_ref[...], wu_ref[...], preferred_element_type=F32), 0.0)
    o_ref[...] += jnp.dot((u * u).astype(BF16), wd_ref[...], preferred_element_type=F32)

    if final_norm:
        @pl.when(j == pl.num_programs(1) - 1)
        def _():
            o_ref[...] = _rms_scale(o_ref[...], g_ref[...])


def _mlp(h, wu, wd, x2, g, tm, tf, final_norm):
    t, d = x2.shape
    f = wu.shape[1]
    return pl.pallas_call(
        functools.partial(_mlp_kernel, final_norm=final_norm),
        grid=(t // tm, f // tf),
        in_specs=[
            pl.BlockSpec((tm, d), lambda i, j: (i, 0)),
            pl.BlockSpec((d, tf), lambda i, j: (0, j)),
            pl.BlockSpec((tf, d), lambda i, j: (j, 0)),
            pl.BlockSpec((tm, d), lambda i, j: (i, 0)),
            pl.BlockSpec((1, d), lambda i, j: (0, 0)),
        ],
        out_specs=pl.BlockSpec((tm, d), lambda i, j: (i, 0)),
        out_shape=jax.ShapeDtypeStruct((t, d), F32),
        compiler_params=_params(("parallel", "arbitrary")),
        name="mlp",
    )(h, wu, wd, x2, g)


def _tile(n, want):
    t = min(n, want)
    while n % t:
        t //= 2
    return t


def kernel(x, norm_mix_g, w_in, b_forget, w_out_fox, w_out_sb, w_out, norm_mlp_g, w_mlp_up,
           w_mlp_down, norm_final_g):
    batch, seq, d = x.shape
    depth = w_in.shape[0]
    n_heads_fox = b_forget.shape[-1]
    width_fox = w_out_fox.shape[1]
    width_sb = w_out_sb.shape[1]
    n_heads_sb = width_sb // HEAD_DIM
    assert width_fox == n_heads_fox * HEAD_DIM and n_heads_fox == n_heads_sb
    assert n_heads_fox <= 8 and seq % 2048 == 0 and d % V7X_LANES == 0
    t = batch * seq
    x2d = x.reshape(t, d)

    for l in range(depth):
        w = w_in[l].astype(BF16)
        o_f = 3 * width_fox
        o_sb = o_f + n_heads_fox
        o_g = o_sb + 3 * width_sb
        w_qkv_fox = w[:, :o_f]
        w_qkv_sb = w[:, o_sb:o_g]
        w_f = jnp.pad(w[:, o_f:o_sb], ((0, 0), (0, V7X_LANES - n_heads_fox)))
        b_f = jnp.pad(b_forget[l], (0, V7X_LANES - n_heads_fox)).reshape(1, V7X_LANES)
        w_g = w[:, o_g:]

        xn, lf = _norm_forget(x2d, norm_mix_g[l].reshape(1, d), w_f, b_f, _tile(t, 1024))
        c = _forget_cumsum(lf.reshape(batch, seq, V7X_LANES), n_heads_fox, _tile(seq, 512))
        qkv = _proj_heads(xn, w_qkv_fox, w_qkv_sb, batch, n_heads_fox, _tile(seq, 2048))
        gates = _proj_gates(xn, w_g, _tile(t, 2048), _tile(2 * d, 1024))

        ya = _fox_attention(qkv, c, 512, 4).reshape(t, width_fox)
        yb = _sb_attention(qkv, 2048, 256).reshape(t, width_sb)

        x2d, h = _merge_project(ya, yb, gates, x2d, w_out_fox[l].astype(BF16),
                                w_out_sb[l].astype(BF16), w_out[l].astype(BF16),
                                norm_mlp_g[l].reshape(1, d), _tile(t, 512))
        x2d = _mlp(h, w_mlp_up[l].astype(BF16), w_mlp_down[l].astype(BF16), x2d,
                   norm_final_g.reshape(1, d), _tile(t, 512), _tile(w_mlp_up.shape[2], 2048),
                   final_norm=(l == depth - 1))
    return x2d.reshape(batch, seq, d)
```

```python
import functools

import jax
import jax.numpy as jnp
from jax import lax
from jax.experimental import pallas as pl
from jax.experimental.pallas import tpu as pltpu

HEAD_DIM = 128
RMS_EPS = 1e-6
MASKED_LOGIT = -1e30
EXP_IS_ZERO_BELOW = -104.0
LOG2_E = 1.4426950408889634
V7X_LANES = 128
V7X_VMEM_LIMIT_BYTES = 60 * 1024 * 1024

F32 = jnp.float32
BF16 = jnp.bfloat16


def _params(semantics, vmem_bytes=V7X_VMEM_LIMIT_BYTES):
    return pltpu.CompilerParams(dimension_semantics=semantics, vmem_limit_bytes=vmem_bytes)


def _log_sigmoid(u):
    return jnp.minimum(u, 0.0) - jnp.log1p(jnp.exp(-jnp.abs(u)))


def _rms_scale(x, g):
    ms = jnp.mean(x * x, axis=-1, keepdims=True)
    return x * lax.rsqrt(ms + RMS_EPS) * g


def _norm_forget_kernel(x_ref, g_ref, wf_ref, bf_ref, xn_ref, lf_ref):
    xn = _rms_scale(x_ref[...], g_ref[...]).astype(BF16)
    xn_ref[...] = xn
    f = jnp.dot(xn, wf_ref[...], preferred_element_type=F32) + bf_ref[...]
    lf_ref[...] = _log_sigmoid(f)


def _norm_forget(x2d, g, wf, bf, tm):
    t, d = x2d.shape
    return pl.pallas_call(
        _norm_forget_kernel,
        grid=(t // tm,),
        in_specs=[
            pl.BlockSpec((tm, d), lambda i: (i, 0)),
            pl.BlockSpec((1, d), lambda i: (0, 0)),
            pl.BlockSpec((d, V7X_LANES), lambda i: (0, 0)),
            pl.BlockSpec((1, V7X_LANES), lambda i: (0, 0)),
        ],
        out_specs=[
            pl.BlockSpec((tm, d), lambda i: (i, 0)),
            pl.BlockSpec((tm, V7X_LANES), lambda i: (i, 0)),
        ],
        out_shape=[
            jax.ShapeDtypeStruct((t, d), BF16),
            jax.ShapeDtypeStruct((t, V7X_LANES), F32),
        ],
        compiler_params=_params(("parallel",)),
        name="norm_forget",
    )(x2d, g, wf, bf)


def _cumsum_kernel(lf_ref, c_ref, carry_ref, *, n_heads):
    @pl.when(pl.program_id(1) == 0)
    def _():
        carry_ref[...] = jnp.zeros_like(carry_ref)

    tc = lf_ref.shape[1]
    lft = lf_ref[0].T[:n_heads, :]
    row = lax.broadcasted_iota(jnp.int32, (tc, tc), 0)
    col = lax.broadcasted_iota(jnp.int32, (tc, tc), 1)
    upper = (row <= col).astype(BF16)
    cs = carry_ref[:, :1]
    rest = lft
    for _ in range(3):
        piece = rest.astype(BF16)
        cs = cs + jnp.dot(piece, upper, preferred_element_type=F32)
        rest = rest - piece.astype(F32)
    c_ref[0] = cs
    carry_ref[...] = jnp.broadcast_to(cs[:, tc - 1:tc], carry_ref.shape)


def _forget_cumsum(lf, n_heads, tc):
    b, s, _ = lf.shape
    return pl.pallas_call(
        functools.partial(_cumsum_kernel, n_heads=n_heads),
        grid=(b, s // tc),
        in_specs=[pl.BlockSpec((1, tc, V7X_LANES), lambda i, j: (i, j, 0))],
        out_specs=pl.BlockSpec((1, n_heads, tc), lambda i, j: (i, 0, j)),
        out_shape=jax.ShapeDtypeStruct((b, n_heads, s), F32),
        scratch_shapes=[pltpu.VMEM((n_heads, V7X_LANES), F32)],
        compiler_params=_params(("arbitrary", "arbitrary")),
        name="forget_cumsum",
    )(lf)


def _proj_heads_kernel(x_ref, wa_ref, wb_ref, o_ref, *, groups_a):
    j = pl.program_id(1)

    def emit(w_ref):
        acc = jnp.dot(x_ref[...], w_ref[...], preferred_element_type=F32)
        for hh in range(o_ref.shape[2]):
            o_ref[0, 0, hh] = acc[:, hh * HEAD_DIM:(hh + 1) * HEAD_DIM].astype(o_ref.dtype)

    @pl.when(j < groups_a)
    def _():
        emit(wa_ref)

    @pl.when(j >= groups_a)
    def _():
        emit(wb_ref)


def _proj_heads(xn, wa, wb, batch, n_heads, tm):
    t, d = xn.shape
    tn = n_heads * HEAD_DIM
    ga, gb = wa.shape[1] // tn, wb.shape[1] // tn
    s = t // batch
    nst = s // tm
    return pl.pallas_call(
        functools.partial(_proj_heads_kernel, groups_a=ga),
        grid=(t // tm, ga + gb),
        in_specs=[
            pl.BlockSpec((tm, d), lambda i, j: (i, 0)),
            pl.BlockSpec((d, tn), lambda i, j: (0, jnp.minimum(j, ga - 1))),
            pl.BlockSpec((d, tn), lambda i, j: (0, jnp.maximum(j - ga, 0))),
        ],
        out_specs=pl.BlockSpec((1, 1, n_heads, tm, HEAD_DIM),
                               lambda i, j: (j, i // nst, 0, i % nst, 0)),
        out_shape=jax.ShapeDtypeStruct((ga + gb, batch, n_heads, s, HEAD_DIM), BF16),
        compiler_params=_params(("parallel", "arbitrary")),
        name="proj_heads",
    )(xn, wa, wb)


def _proj_gate_kernel(x_ref, w_ref, o_ref):
    acc = jnp.dot(x_ref[...], w_ref[...], preferred_element_type=F32)
    o_ref[...] = (0.5 * jnp.tanh(0.5 * acc) + 0.5).astype(o_ref.dtype)


def _proj_gates(xn, w, tm, tn):
    t, d = xn.shape
    n = w.shape[1]
    return pl.pallas_call(
        _proj_gate_kernel,
        grid=(t // tm, n // tn),
        in_specs=[
            pl.BlockSpec((tm, d), lambda i, j: (i, 0)),
            pl.BlockSpec((d, tn), lambda i, j: (0, j)),
        ],
        out_specs=pl.BlockSpec((tm, tn), lambda i, j: (i, j)),
        out_shape=jax.ShapeDtypeStruct((t, n), BF16),
        compiler_params=_params(("parallel", "arbitrary")),
        name="proj_gates",
    )(xn, w)


def _fox_kernel(cstart_ref, cend_ref, q_ref, k_ref, v_ref, c_ref, o_ref, kn2_ref, *,
                n_heads, blk, scale):
    b, h, step = pl.program_id(0), pl.program_id(1), pl.program_id(2)
    bh = b * n_heads + h
    tiles = q_ref.shape[3] // blk
    key = lax.broadcasted_iota(jnp.int32, (blk, blk), 0)
    query = lax.broadcasted_iota(jnp.int32, (blk, blk), 1)
    ones_rows = jnp.ones((8, HEAD_DIM), BF16)

    @pl.when(step == 0)
    def _():
        kn2_ref[0] = jnp.float32(0.0)

    kf = k_ref[0, 0, 0, pl.ds(step * (tiles * blk), tiles * blk), :].astype(F32)
    kn2_ref[0] = jnp.maximum(kn2_ref[0], jnp.max(jnp.sum(kf * kf, axis=-1)))
    kn2 = kn2_ref[0]

    def logits_t(q2, j, shift):
        kblk = k_ref[0, 0, 0, pl.ds(j * blk, blk), :]
        s = lax.dot_general(kblk, q2, (((1,), (1,)), ((), ())), preferred_element_type=F32)
        bias_row = (shift - c_ref[0, j]) * LOG2_E
        bias_col = jnp.broadcast_to(bias_row, (V7X_LANES, blk)).T
        return s + jnp.concatenate([bias_col] * (blk // V7X_LANES), axis=1)

    def weighted_values_t(p_t, j):
        vblk = v_ref[0, 0, 0, pl.ds(j * blk, blk), :]
        return lax.dot_general(vblk, p_t.astype(BF16), (((0,), (0,)), ((), ())),
                               preferred_element_type=F32)

    def near_pass(t):
        qi = step * tiles + t
        c0 = cstart_ref[bh, qi]
        q2 = (q_ref[0, 0, 0, t * blk:(t + 1) * blk, :].astype(F32)
              * (scale * LOG2_E)).astype(BF16)
        j_prev = jnp.maximum(qi - 1, 0)
        s_prev = logits_t(q2, j_prev, c0 + jnp.where(qi >= 1, 0.0, MASKED_LOGIT))
        s_diag = jnp.where(key <= query, logits_t(q2, qi, c0), MASKED_LOGIT)
        m = jnp.maximum(jnp.max(s_prev, axis=0, keepdims=True),
                        jnp.max(s_diag, axis=0, keepdims=True))
        p_prev = jnp.exp2(s_prev - m)
        p_diag = jnp.exp2(s_diag - m)
        l = jnp.sum(p_prev, axis=0, keepdims=True) + jnp.sum(p_diag, axis=0, keepdims=True)
        acc = weighted_values_t(p_prev, j_prev) + weighted_values_t(p_diag, qi)
        q2f = q2.astype(F32)
        qn2 = lax.dot_general(ones_rows, (q2f * q2f).astype(BF16), (((1,), (1,)), ((), ())),
                              preferred_element_type=F32)[:1]
        excess = jnp.max(jnp.sqrt(qn2 * (1.01 * kn2)) - m)
        return qi, c0, q2, m, l, acc, excess

    def far_pass(qi, c0, q2, m, l, acc, excess):
        def forget_gap(j):
            return (c0 - cend_ref[bh, jnp.maximum(j, 0)]) * LOG2_E

        def count_cond(n):
            j = qi - 2 - n
            return jnp.logical_and(j >= 0,
                                   excess + forget_gap(j) >= EXP_IS_ZERO_BELOW * LOG2_E)

        n_blocks = lax.while_loop(count_cond, lambda n: n + 1, jnp.int32(0))

        def general_body(n, carry):
            m, l, acc = carry
            j = qi - 2 - n
            s = logits_t(q2, j, c0)
            m_new = jnp.maximum(m, jnp.max(s, axis=0, keepdims=True))
            p = jnp.exp2(s - m_new)
            alpha = jnp.exp2(m - m_new)
            l = alpha * l + jnp.sum(p, axis=0, keepdims=True)
            acc = alpha * acc + weighted_values_t(p, j)
            return m_new, l, acc

        def general_loop(_):
            return lax.fori_loop(0, n_blocks, general_body, (m, l, acc))[1:]

        def fixed_max_body(n, carry):
            l, acc = carry
            ja = qi - 2 - 2 * n
            p_a = jnp.exp2(logits_t(q2, ja, c0) - m)
            p_b = jnp.exp2(logits_t(q2, ja - 1, c0) - m)
            l = l + jnp.sum(p_a, axis=0, keepdims=True) + jnp.sum(p_b, axis=0, keepdims=True)
            acc = acc + weighted_values_t(p_a, ja) + weighted_values_t(p_b, ja - 1)
            return l, acc

        def fixed_max_loop(_):
            l2, acc2 = lax.fori_loop(0, lax.shift_right_logical(n_blocks, jnp.int32(1)), fixed_max_body,
                                     (l, acc))

            def last_block(_):
                j = qi - 1 - n_blocks
                p = jnp.exp2(logits_t(q2, j, c0) - m)
                return l2 + jnp.sum(p, axis=0, keepdims=True), acc2 + weighted_values_t(p, j)

            return lax.cond(jnp.bitwise_and(n_blocks, 1) == 1, last_block,
                            lambda _: (l2, acc2), None)

        max_is_final = excess + forget_gap(qi - 2) <= 0.0
        l, acc = lax.cond(max_is_final, fixed_max_loop, general_loop, None)
        return (acc / l).T

    near = [near_pass(t) for t in range(tiles)]
    for t, state in enumerate(near):
        o_ref[0, t * blk:(t + 1) * blk, :] = far_pass(*state).astype(o_ref.dtype)


def _fox_attention(qkv, c, blk, tiles):
    _, batch, n_heads, seq, _ = qkv.shape
    nb = seq // blk
    rows = tiles * blk
    cflat = c.reshape(batch * n_heads, nb, blk)
    cstart = cflat[:, :, 0]
    cend = cflat[:, :, blk - 1]
    cblocks = cflat.reshape(batch * n_heads, nb, 1, blk)
    kernel = functools.partial(_fox_kernel, n_heads=n_heads, blk=blk, scale=HEAD_DIM ** -0.5)
    grid_spec = pltpu.PrefetchScalarGridSpec(
        num_scalar_prefetch=2,
        grid=(batch, n_heads, seq // rows),
        in_specs=[
            pl.BlockSpec((1, 1, 1, rows, HEAD_DIM), lambda b, h, i, *_: (0, b, h, i, 0)),
            pl.BlockSpec((1, 1, 1, seq, HEAD_DIM), lambda b, h, i, *_: (1, b, h, 0, 0)),
            pl.BlockSpec((1, 1, 1, seq, HEAD_DIM), lambda b, h, i, *_: (2, b, h, 0, 0)),
            pl.BlockSpec((1, nb, 1, blk), lambda b, h, i, *_: (b * n_heads + h, 0, 0, 0)),
        ],
        out_specs=pl.BlockSpec((1, rows, HEAD_DIM), lambda b, h, i, *_: (b, i, h)),
        scratch_shapes=[pltpu.SMEM((1,), F32)],
    )
    return pl.pallas_call(
        kernel,
        grid_spec=grid_spec,
        out_shape=jax.ShapeDtypeStruct((batch, seq, n_heads * HEAD_DIM), BF16),
        compiler_params=_params(("parallel", "parallel", "arbitrary")),
        name="fox_attention",
    )(cstart, cend, qkv, qkv, qkv, cblocks)


def _sb_kernel(q_ref, k_ref, v_ref, o_ref, *, sub, scale):
    qi = pl.program_id(2)
    n_sub = q_ref.shape[3] // sub
    row = lax.broadcasted_iota(jnp.int32, (sub, sub), 0)
    col = lax.broadcasted_iota(jnp.int32, (sub, sub), 1)
    later = (row > col).astype(BF16)
    strict = col < row

    def block(y_q, j, carry, acc, masked, y_bias=None):
        kblk = k_ref[0, 0, 0, pl.ds(j * sub, sub), :]
        vblk = v_ref[0, 0, 0, pl.ds(j * sub, sub), :]
        y = lax.dot_general(y_q, kblk, (((1,), (1,)), ((), ())), preferred_element_type=F32)
        if y_bias is not None:
            y = y + y_bias
        if masked:
            y = jnp.where(strict, y, -MASKED_LOGIT)
        log2_not_beta = jnp.minimum(y, 0.0) - jnp.log2(1.0 + jnp.exp2(-jnp.abs(y)))
        after = jnp.dot(log2_not_beta.astype(BF16), later, preferred_element_type=F32) + carry
        a = jnp.exp2(log2_not_beta - y + after)
        acc = acc + jnp.dot(a.astype(BF16), vblk, preferred_element_type=F32)
        carry = carry + jnp.sum(log2_not_beta, axis=-1, keepdims=True)
        return carry, acc

    states = []
    for t in range(n_sub):
        y_q = (q_ref[0, 0, 0, t * sub:(t + 1) * sub, :].astype(F32)
               * (-scale * LOG2_E)).astype(BF16)
        jd = qi * n_sub + t
        carry, acc = block(y_q, jd, jnp.zeros((sub, 1), F32), jnp.zeros((sub, HEAD_DIM), F32),
                           True)
        if t == 0:
            carry, acc = block(y_q, jnp.maximum(jd - 1, 0), carry, acc, False,
                               y_bias=jnp.where(jd >= 1, 0.0, -MASKED_LOGIT))
        else:
            carry, acc = block(y_q, jd - 1, carry, acc, False)
        states.append((y_q, jd, carry, acc))

    def walk_back(_):
        accs = []
        for y_q, jd, carry, acc in states:
            def cond(state):
                j, carry, _ = state
                return jnp.logical_and(j >= 0, jnp.max(carry) >= EXP_IS_ZERO_BELOW * LOG2_E)

            def body(state, y_q=y_q):
                j, carry, acc = state
                carry, acc = block(y_q, j, carry, acc, False)
                return j - 1, carry, acc

            accs.append(lax.while_loop(cond, body, (jd - 2, carry, acc))[2])
        return accs

    highest_carry = jnp.max(functools.reduce(jnp.maximum, [s[2] for s in states]))
    accs = lax.cond(highest_carry >= EXP_IS_ZERO_BELOW * LOG2_E, walk_back,
                    lambda _: [s[3] for s in states], None)
    for t, acc in enumerate(accs):
        o_ref[0, t * sub:(t + 1) * sub, :] = acc.astype(o_ref.dtype)


def _sb_attention(qkv, blk, sub):
    _, batch, n_heads, seq, _ = qkv.shape
    nb = seq // blk
    kernel = functools.partial(_sb_kernel, sub=sub, scale=HEAD_DIM ** -0.5)
    return pl.pallas_call(
        kernel,
        grid=(batch, n_heads, nb),
        in_specs=[
            pl.BlockSpec((1, 1, 1, blk, HEAD_DIM), lambda b, h, i: (3, b, h, i, 0)),
            pl.BlockSpec((1, 1, 1, seq, HEAD_DIM), lambda b, h, i: (4, b, h, 0, 0)),
            pl.BlockSpec((1, 1, 1, seq, HEAD_DIM), lambda b, h, i: (5, b, h, 0, 0)),
        ],
        out_specs=pl.BlockSpec((1, blk, HEAD_DIM), lambda b, h, i: (b, i, h)),
        out_shape=jax.ShapeDtypeStruct((batch, seq, n_heads * HEAD_DIM), BF16),
        compiler_params=_params(("parallel", "parallel", "arbitrary")),
        name="sb_attention",
    )(qkv, qkv, qkv)


def _merge_kernel(ya_ref, yb_ref, ga_ref, gb_ref, x_ref, wa_ref, wb_ref, wo_ref, g_ref,
                  x2_ref, h_ref):
    ya = jnp.dot(ya_ref[...], wa_ref[...], preferred_element_type=F32)
    yb = jnp.dot(yb_ref[...], wb_ref[...], preferred_element_type=F32)
    merged = ga_ref[...].astype(F32) * ya + gb_ref[...].astype(F32) * yb
    x2 = x_ref[...] + jnp.dot(merged.astype(BF16), wo_ref[...], preferred_element_type=F32)
    x2_ref[...] = x2
    h_ref[...] = _rms_scale(x2, g_ref[...]).astype(h_ref.dtype)


def _merge_project(ya, yb, gates, x2d, wa, wb, wo, g, tm):
    t, d = x2d.shape
    wa_w = ya.shape[1]
    wb_w = yb.shape[1]
    resident = lambda shape: pl.BlockSpec(shape, lambda i: (0, 0), pipeline_mode=pl.Buffered(1))
    return pl.pallas_call(
        _merge_kernel,
        grid=(t // tm,),
        in_specs=[
            pl.BlockSpec((tm, wa_w), lambda i: (i, 0)),
            pl.BlockSpec((tm, wb_w), lambda i: (i, 0)),
            pl.BlockSpec((tm, d), lambda i: (i, 0)),
            pl.BlockSpec((tm, d), lambda i: (i, 1)),
            pl.BlockSpec((tm, d), lambda i: (i, 0)),
            resident((wa_w, d)),
            resident((wb_w, d)),
            resident((d, d)),
            resident((1, d)),
        ],
        out_specs=[
            pl.BlockSpec((tm, d), lambda i: (i, 0)),
            pl.BlockSpec((tm, d), lambda i: (i, 0)),
        ],
        out_shape=[
            jax.ShapeDtypeStruct((t, d), F32),
            jax.ShapeDtypeStruct((t, d), BF16),
        ],
        compiler_params=_params(("parallel",)),
        name="merge_project",
    )(ya, yb, gates, gates, x2d, wa, wb, wo, g)


def _mlp_kernel(h_ref, wu_ref, wd_ref, x2_ref, g_ref, o_ref, *, final_norm):
    j = pl.program_id(1)

    @pl.when(j == 0)
    def _():
        o_ref[...] = x2_ref[...]

    u = jnp.maximum(jnp.dot(h_ref[...], wu_ref[...], preferred_element_type=F32), 0.0)
    o_ref[...] += jnp.dot((u * u).astype(BF16), wd_ref[...], preferred_element_type=F32)

    if final_norm:
        @pl.when(j == pl.num_programs(1) - 1)
        def _():
            o_ref[...] = _rms_scale(o_ref[...], g_ref[...])


def _mlp(h, wu, wd, x2, g, tm, tf, final_norm):
    t, d = x2.shape
    f = wu.shape[1]
    return pl.pallas_call(
        functools.partial(_mlp_kernel, final_norm=final_norm),
        grid=(t // tm, f // tf),
        in_specs=[
            pl.BlockSpec((tm, d), lambda i, j: (i, 0)),
            pl.BlockSpec((d, tf), lambda i, j: (0, j)),
            pl.BlockSpec((tf, d), lambda i, j: (j, 0)),
            pl.BlockSpec((tm, d), lambda i, j: (i, 0)),
            pl.BlockSpec((1, d), lambda i, j: (0, 0)),
        ],
        out_specs=pl.BlockSpec((tm, d), lambda i, j: (i, 0)),
        out_shape=jax.ShapeDtypeStruct((t, d), F32),
        compiler_params=_params(("parallel", "arbitrary")),
        name="mlp",
    )(h, wu, wd, x2, g)


def _tile(n, want):
    t = min(n, want)
    while n % t:
        t //= 2
    return t


def kernel(x, norm_mix_g, w_in, b_forget, w_out_fox, w_out_sb, w_out, norm_mlp_g, w_mlp_up,
           w_mlp_down, norm_final_g):
    batch, seq, d = x.shape
    depth = w_in.shape[0]
    n_heads_fox = b_forget.shape[-1]
    width_fox = w_out_fox.shape[1]
    width_sb = w_out_sb.shape[1]
    n_heads_sb = width_sb // HEAD_DIM
    assert width_fox == n_heads_fox * HEAD_DIM and n_heads_fox == n_heads_sb
    assert n_heads_fox <= 8 and seq % 4096 == 0 and d % V7X_LANES == 0
    t = batch * seq
    x2d = x.reshape(t, d)

    for l in range(depth):
        w = w_in[l].astype(BF16)
        o_f = 3 * width_fox
        o_sb = o_f + n_heads_fox
        o_g = o_sb + 3 * width_sb
        w_qkv_fox = w[:, :o_f]
        w_qkv_sb = w[:, o_sb:o_g]
        w_f = jnp.pad(w[:, o_f:o_sb], ((0, 0), (0, V7X_LANES - n_heads_fox)))
        b_f = jnp.pad(b_forget[l], (0, V7X_LANES - n_heads_fox)).reshape(1, V7X_LANES)
        w_g = w[:, o_g:]

        xn, lf = _norm_forget(x2d, norm_mix_g[l].reshape(1, d), w_f, b_f, _tile(t, 1024))
        c = _forget_cumsum(lf.reshape(batch, seq, V7X_LANES), n_heads_fox, _tile(seq, 512))
        qkv = _proj_heads(xn, w_qkv_fox, w_qkv_sb, batch, n_heads_fox, _tile(seq, 2048))
        gates = _proj_gates(xn, w_g, _tile(t, 2048), _tile(2 * d, 1024))

        ya = _fox_attention(qkv, c, 512, 8).reshape(t, width_fox)
        yb = _sb_attention(qkv, 4096, 256).reshape(t, width_sb)

        x2d, h = _merge_project(ya, yb, gates, x2d, w_out_fox[l].astype(BF16),
                                w_out_sb[l].astype(BF16), w_out[l].astype(BF16),
                                norm_mlp_g[l].reshape(1, d), _tile(t, 512))
        x2d = _mlp(h, w_mlp_up[l].astype(BF16), w_mlp_down[l].astype(BF16), x2d,
                   norm_final_g.reshape(1, d), _tile(t, 512), _tile(w_mlp_up.shape[2], 2048),
                   final_norm=(l == depth - 1))
    return x2d.reshape(batch, seq, d)
```

```python
import functools

import jax
import jax.numpy as jnp
from jax import lax
from jax.experimental import pallas as pl
from jax.experimental.pallas import tpu as pltpu

HEAD_DIM = 128
RMS_EPS = 1e-6
MASKED_LOGIT = -1e30
EXP_IS_ZERO_BELOW = -104.0
LOG2_E = 1.4426950408889634
V7X_LANES = 128
V7X_VMEM_LIMIT_BYTES = 60 * 1024 * 1024

F32 = jnp.float32
BF16 = jnp.bfloat16


def _params(semantics, vmem_bytes=V7X_VMEM_LIMIT_BYTES):
    return pltpu.CompilerParams(dimension_semantics=semantics, vmem_limit_bytes=vmem_bytes)


def _log_sigmoid(u):
    return jnp.minimum(u, 0.0) - jnp.log1p(jnp.exp(-jnp.abs(u)))


def _rms_scale(x, g):
    ms = jnp.mean(x * x, axis=-1, keepdims=True)
    return x * lax.rsqrt(ms + RMS_EPS) * g


def _norm_forget_kernel(x_ref, g_ref, wf_ref, bf_ref, xn_ref, lf_ref):
    xn = _rms_scale(x_ref[...], g_ref[...]).astype(BF16)
    xn_ref[...] = xn
    f = jnp.dot(xn, wf_ref[...], preferred_element_type=F32) + bf_ref[...]
    lf_ref[...] = _log_sigmoid(f)


def _norm_forget(x2d, g, wf, bf, tm):
    t, d = x2d.shape
    return pl.pallas_call(
        _norm_forget_kernel,
        grid=(t // tm,),
        in_specs=[
            pl.BlockSpec((tm, d), lambda i: (i, 0)),
            pl.BlockSpec((1, d), lambda i: (0, 0)),
            pl.BlockSpec((d, V7X_LANES), lambda i: (0, 0)),
            pl.BlockSpec((1, V7X_LANES), lambda i: (0, 0)),
        ],
        out_specs=[
            pl.BlockSpec((tm, d), lambda i: (i, 0)),
            pl.BlockSpec((tm, V7X_LANES), lambda i: (i, 0)),
        ],
        out_shape=[
            jax.ShapeDtypeStruct((t, d), BF16),
            jax.ShapeDtypeStruct((t, V7X_LANES), F32),
        ],
        compiler_params=_params(("parallel",)),
        name="norm_forget",
    )(x2d, g, wf, bf)


def _cumsum_kernel(lf_ref, c_ref, carry_ref, *, n_heads):
    @pl.when(pl.program_id(1) == 0)
    def _():
        carry_ref[...] = jnp.zeros_like(carry_ref)

    tc = lf_ref.shape[1]
    lft = lf_ref[0].T[:n_heads, :]
    row = lax.broadcasted_iota(jnp.int32, (tc, tc), 0)
    col = lax.broadcasted_iota(jnp.int32, (tc, tc), 1)
    upper = (row <= col).astype(BF16)
    cs = carry_ref[:, :1]
    rest = lft
    for _ in range(3):
        piece = rest.astype(BF16)
        cs = cs + jnp.dot(piece, upper, preferred_element_type=F32)
        rest = rest - piece.astype(F32)
    c_ref[0] = cs
    carry_ref[...] = jnp.broadcast_to(cs[:, tc - 1:tc], carry_ref.shape)


def _forget_cumsum(lf, n_heads, tc):
    b, s, _ = lf.shape
    return pl.pallas_call(
        functools.partial(_cumsum_kernel, n_heads=n_heads),
        grid=(b, s // tc),
        in_specs=[pl.BlockSpec((1, tc, V7X_LANES), lambda i, j: (i, j, 0))],
        out_specs=pl.BlockSpec((1, n_heads, tc), lambda i, j: (i, 0, j)),
        out_shape=jax.ShapeDtypeStruct((b, n_heads, s), F32),
        scratch_shapes=[pltpu.VMEM((n_heads, V7X_LANES), F32)],
        compiler_params=_params(("arbitrary", "arbitrary")),
        name="forget_cumsum",
    )(lf)


def _proj_heads_kernel(x_ref, wa_ref, wb_ref, o_ref, *, groups_a):
    j = pl.program_id(1)

    def emit(w_ref):
        acc = jnp.dot(x_ref[...], w_ref[...], preferred_element_type=F32)
        for hh in range(o_ref.shape[2]):
            o_ref[0, 0, hh] = acc[:, hh * HEAD_DIM:(hh + 1) * HEAD_DIM].astype(o_ref.dtype)

    @pl.when(j < groups_a)
    def _():
        emit(wa_ref)

    @pl.when(j >= groups_a)
    def _():
        emit(wb_ref)


def _proj_heads(xn, wa, wb, batch, n_heads, tm):
    t, d = xn.shape
    tn = n_heads * HEAD_DIM
    ga, gb = wa.shape[1] // tn, wb.shape[1] // tn
    s = t // batch
    nst = s // tm
    return pl.pallas_call(
        functools.partial(_proj_heads_kernel, groups_a=ga),
        grid=(t // tm, ga + gb),
        in_specs=[
            pl.BlockSpec((tm, d), lambda i, j: (i, 0)),
            pl.BlockSpec((d, tn), lambda i, j: (0, jnp.minimum(j, ga - 1))),
            pl.BlockSpec((d, tn), lambda i, j: (0, jnp.maximum(j - ga, 0))),
        ],
        out_specs=pl.BlockSpec((1, 1, n_heads, tm, HEAD_DIM),
                               lambda i, j: (j, i // nst, 0, i % nst, 0)),
        out_shape=jax.ShapeDtypeStruct((ga + gb, batch, n_heads, s, HEAD_DIM), BF16),
        compiler_params=_params(("parallel", "arbitrary")),
        name="proj_heads",
    )(xn, wa, wb)


def _proj_gate_kernel(x_ref, w_ref, o_ref):
    acc = jnp.dot(x_ref[...], w_ref[...], preferred_element_type=F32)
    o_ref[...] = (0.5 * jnp.tanh(0.5 * acc) + 0.5).astype(o_ref.dtype)


def _proj_gates(xn, w, tm, tn):
    t, d = xn.shape
    n = w.shape[1]
    return pl.pallas_call(
        _proj_gate_kernel,
        grid=(t // tm, n // tn),
        in_specs=[
            pl.BlockSpec((tm, d), lambda i, j: (i, 0)),
            pl.BlockSpec((d, tn), lambda i, j: (0, j)),
        ],
        out_specs=pl.BlockSpec((tm, tn), lambda i, j: (i, j)),
        out_shape=jax.ShapeDtypeStruct((t, n), BF16),
        compiler_params=_params(("parallel", "arbitrary")),
        name="proj_gates",
    )(xn, w)


def _fox_kernel(cstart_ref, cend_ref, q_ref, k_ref, v_ref, c_ref, o_ref, kn2_ref, *,
                n_heads, blk, scale):
    b, h, step = pl.program_id(0), pl.program_id(1), pl.program_id(2)
    bh = b * n_heads + h
    tiles = q_ref.shape[3] // blk
    key = lax.broadcasted_iota(jnp.int32, (blk, blk), 0)
    query = lax.broadcasted_iota(jnp.int32, (blk, blk), 1)
    ones_rows = jnp.ones((8, HEAD_DIM), BF16)

    @pl.when(step == 0)
    def _():
        kn2_ref[0] = jnp.float32(0.0)

    kf = k_ref[0, 0, 0, pl.ds(step * (tiles * blk), tiles * blk), :].astype(F32)
    kn2_ref[0] = jnp.maximum(kn2_ref[0], jnp.max(jnp.sum(kf * kf, axis=-1)))
    kn2 = kn2_ref[0]

    def logits_t(q2, j, shift):
        kblk = k_ref[0, 0, 0, pl.ds(j * blk, blk), :]
        s = lax.dot_general(kblk, q2, (((1,), (1,)), ((), ())), preferred_element_type=F32)
        bias_row = (shift - c_ref[0, j]) * LOG2_E
        bias_col = jnp.broadcast_to(bias_row, (V7X_LANES, blk)).T
        return s + jnp.concatenate([bias_col] * (blk // V7X_LANES), axis=1)

    def weighted_values_t(p_t, j):
        vblk = v_ref[0, 0, 0, pl.ds(j * blk, blk), :]
        return lax.dot_general(vblk, p_t.astype(BF16), (((0,), (0,)), ((), ())),
                               preferred_element_type=F32)

    def near_pass(t):
        qi = step * tiles + t
        c0 = cstart_ref[bh, qi]
        q2 = (q_ref[0, 0, 0, t * blk:(t + 1) * blk, :].astype(F32)
              * (scale * LOG2_E)).astype(BF16)
        j_prev = jnp.maximum(qi - 1, 0)
        s_prev = logits_t(q2, j_prev, c0 + jnp.where(qi >= 1, 0.0, MASKED_LOGIT))
        s_diag = jnp.where(key <= query, logits_t(q2, qi, c0), MASKED_LOGIT)
        m = jnp.maximum(jnp.max(s_prev, axis=0, keepdims=True),
                        jnp.max(s_diag, axis=0, keepdims=True))
        p_prev = jnp.exp2(s_prev - m)
        p_diag = jnp.exp2(s_diag - m)
        l = jnp.sum(p_prev, axis=0, keepdims=True) + jnp.sum(p_diag, axis=0, keepdims=True)
        acc = weighted_values_t(p_prev, j_prev) + weighted_values_t(p_diag, qi)
        q2f = q2.astype(F32)
        qn2 = lax.dot_general(ones_rows, (q2f * q2f).astype(BF16), (((1,), (1,)), ((), ())),
                              preferred_element_type=F32)[:1]
        excess = jnp.max(jnp.sqrt(qn2 * (1.01 * kn2)) - m)
        return qi, c0, q2, m, l, acc, excess

    def far_pass(qi, c0, q2, m, l, acc, excess):
        def forget_gap(j):
            return (c0 - cend_ref[bh, jnp.maximum(j, 0)]) * LOG2_E

        def count_cond(n):
            j = qi - 2 - n
            return jnp.logical_and(j >= 0,
                                   excess + forget_gap(j) >= EXP_IS_ZERO_BELOW * LOG2_E)

        n_blocks = lax.while_loop(count_cond, lambda n: n + 1, jnp.int32(0))

        def general_body(n, carry):
            m, l, acc = carry
            j = qi - 2 - n
            s = logits_t(q2, j, c0)
            m_new = jnp.maximum(m, jnp.max(s, axis=0, keepdims=True))
            p = jnp.exp2(s - m_new)
            alpha = jnp.exp2(m - m_new)
            l = alpha * l + jnp.sum(p, axis=0, keepdims=True)
            acc = alpha * acc + weighted_values_t(p, j)
            return m_new, l, acc

        def general_loop(_):
            return lax.fori_loop(0, n_blocks, general_body, (m, l, acc))[1:]

        def fixed_max_body(n, carry):
            l, acc = carry
            ja = qi - 2 - 2 * n
            p_a = jnp.exp2(logits_t(q2, ja, c0) - m)
            p_b = jnp.exp2(logits_t(q2, ja - 1, c0) - m)
            l = l + jnp.sum(p_a, axis=0, keepdims=True) + jnp.sum(p_b, axis=0, keepdims=True)
            acc = acc + weighted_values_t(p_a, ja) + weighted_values_t(p_b, ja - 1)
            return l, acc

        def fixed_max_loop(_):
            l2, acc2 = lax.fori_loop(0, lax.shift_right_logical(n_blocks, jnp.int32(1)), fixed_max_body,
                                     (l, acc))

            def last_block(_):
                j = qi - 1 - n_blocks
                p = jnp.exp2(logits_t(q2, j, c0) - m)
                return l2 + jnp.sum(p, axis=0, keepdims=True), acc2 + weighted_values_t(p, j)

            return lax.cond(jnp.bitwise_and(n_blocks, 1) == 1, last_block,
                            lambda _: (l2, acc2), None)

        max_is_final = excess + forget_gap(qi - 2) <= 0.0
        l, acc = lax.cond(max_is_final, fixed_max_loop, general_loop, None)
        return (acc / l).T

    near = [near_pass(t) for t in range(tiles)]
    for t, state in enumerate(near):
        o_ref[0, t * blk:(t + 1) * blk, :] = far_pass(*state).astype(o_ref.dtype)


def _fox_attention(qkv, c, blk, tiles):
    _, batch, n_heads, seq, _ = qkv.shape
    nb = seq // blk
    rows = tiles * blk
    cflat = c.reshape(batch * n_heads, nb, blk)
    cstart = cflat[:, :, 0]
    cend = cflat[:, :, blk - 1]
    cblocks = cflat.reshape(batch * n_heads, nb, 1, blk)
    kernel = functools.partial(_fox_kernel, n_heads=n_heads, blk=blk, scale=HEAD_DIM ** -0.5)
    grid_spec = pltpu.PrefetchScalarGridSpec(
        num_scalar_prefetch=2,
        grid=(batch, n_heads, seq // rows),
        in_specs=[
            pl.BlockSpec((1, 1, 1, rows, HEAD_DIM), lambda b, h, i, *_: (0, b, h, i, 0)),
            pl.BlockSpec((1, 1, 1, seq, HEAD_DIM), lambda b, h, i, *_: (1, b, h, 0, 0)),
            pl.BlockSpec((1, 1, 1, seq, HEAD_DIM), lambda b, h, i, *_: (2, b, h, 0, 0)),
            pl.BlockSpec((1, nb, 1, blk), lambda b, h, i, *_: (b * n_heads + h, 0, 0, 0)),
        ],
        out_specs=pl.BlockSpec((1, rows, HEAD_DIM), lambda b, h, i, *_: (b, i, h)),
        scratch_shapes=[pltpu.SMEM((1,), F32)],
    )
    return pl.pallas_call(
        kernel,
        grid_spec=grid_spec,
        out_shape=jax.ShapeDtypeStruct((batch, seq, n_heads * HEAD_DIM), BF16),
        compiler_params=_params(("parallel", "parallel", "arbitrary")),
        name="fox_attention",
    )(cstart, cend, qkv, qkv, qkv, cblocks)


def _sb_kernel(q_ref, k_ref, v_ref, o_ref, *, sub, scale):
    qi = pl.program_id(2)
    n_sub = q_ref.shape[3] // sub
    row = lax.broadcasted_iota(jnp.int32, (sub, sub), 0)
    col = lax.broadcasted_iota(jnp.int32, (sub, sub), 1)
    later = (row > col).astype(BF16)
    strict = col < row

    def block(y_q, j, carry, acc, masked, y_bias=None):
        kblk = k_ref[0, 0, 0, pl.ds(j * sub, sub), :]
        vblk = v_ref[0, 0, 0, pl.ds(j * sub, sub), :]
        y = lax.dot_general(y_q, kblk, (((1,), (1,)), ((), ())), preferred_element_type=F32)
        if y_bias is not None:
            y = y + y_bias
        if masked:
            y = jnp.where(strict, y, -MASKED_LOGIT)
        log2_not_beta = jnp.minimum(y, 0.0) - jnp.log2(1.0 + jnp.exp2(-jnp.abs(y)))
        after = jnp.dot(log2_not_beta.astype(BF16), later, preferred_element_type=F32) + carry
        a = jnp.exp2(log2_not_beta - y + after)
        acc = acc + jnp.dot(a.astype(BF16), vblk, preferred_element_type=F32)
        carry = carry + jnp.sum(log2_not_beta, axis=-1, keepdims=True)
        return carry, acc

    def log2_sigmoid(y):
        return jnp.minimum(y, 0.0) - jnp.log2(1.0 + jnp.exp2(-jnp.abs(y)))

    def near_window(y_q, jd):
        kwin = k_ref[0, 0, 0, pl.ds((jd - 1) * sub, 2 * sub), :]
        vwin = v_ref[0, 0, 0, pl.ds((jd - 1) * sub, 2 * sub), :]
        y = lax.dot_general(y_q, kwin, (((1,), (1,)), ((), ())), preferred_element_type=F32)
        y_prev = y[:, :sub]
        y_diag = jnp.where(strict, y[:, sub:], -MASKED_LOGIT)
        lnb_prev = log2_sigmoid(y_prev)
        lnb_diag = log2_sigmoid(y_diag)
        after_diag = jnp.dot(lnb_diag.astype(BF16), later, preferred_element_type=F32)
        carry_diag = jnp.sum(lnb_diag, axis=-1, keepdims=True)
        after_prev = (jnp.dot(lnb_prev.astype(BF16), later, preferred_element_type=F32)
                      + carry_diag)
        a = jnp.concatenate([jnp.exp2(lnb_prev - y_prev + after_prev),
                             jnp.exp2(lnb_diag - y_diag + after_diag)], axis=1)
        acc = jnp.dot(a.astype(BF16), vwin, preferred_element_type=F32)
        return carry_diag + jnp.sum(lnb_prev, axis=-1, keepdims=True), acc

    states = []
    for t in range(n_sub):
        y_q = (q_ref[0, 0, 0, t * sub:(t + 1) * sub, :].astype(F32)
               * (-scale * LOG2_E)).astype(BF16)
        jd = qi * n_sub + t
        if t == 0:
            carry, acc = block(y_q, jd, jnp.zeros((sub, 1), F32),
                               jnp.zeros((sub, HEAD_DIM), F32), True)
            carry, acc = block(y_q, jnp.maximum(jd - 1, 0), carry, acc, False,
                               y_bias=jnp.where(jd >= 1, 0.0, -MASKED_LOGIT))
        else:
            carry, acc = near_window(y_q, jd)
        states.append((y_q, jd, carry, acc))

    def walk_back(_):
        accs = []
        for y_q, jd, carry, acc in states:
            def cond(state):
                j, carry, _ = state
                return jnp.logical_and(j >= 0, jnp.max(carry) >= EXP_IS_ZERO_BELOW * LOG2_E)

            def body(state, y_q=y_q):
                j, carry, acc = state
                carry, acc = block(y_q, j, carry, acc, False)
                return j - 1, carry, acc

            accs.append(lax.while_loop(cond, body, (jd - 2, carry, acc))[2])
        return accs

    highest_carry = jnp.max(functools.reduce(jnp.maximum, [s[2] for s in states]))
    accs = lax.cond(highest_carry >= EXP_IS_ZERO_BELOW * LOG2_E, walk_back,
                    lambda _: [s[3] for s in states], None)
    for t, acc in enumerate(accs):
        o_ref[0, t * sub:(t + 1) * sub, :] = acc.astype(o_ref.dtype)


def _sb_attention(qkv, blk, sub):
    _, batch, n_heads, seq, _ = qkv.shape
    nb = seq // blk
    kernel = functools.partial(_sb_kernel, sub=sub, scale=HEAD_DIM ** -0.5)
    return pl.pallas_call(
        kernel,
        grid=(batch, n_heads, nb),
        in_specs=[
            pl.BlockSpec((1, 1, 1, blk, HEAD_DIM), lambda b, h, i: (3, b, h, i, 0)),
            pl.BlockSpec((1, 1, 1, seq, HEAD_DIM), lambda b, h, i: (4, b, h, 0, 0)),
            pl.BlockSpec((1, 1, 1, seq, HEAD_DIM), lambda b, h, i: (5, b, h, 0, 0)),
        ],
        out_specs=pl.BlockSpec((1, blk, HEAD_DIM), lambda b, h, i: (b, i, h)),
        out_shape=jax.ShapeDtypeStruct((batch, seq, n_heads * HEAD_DIM), BF16),
        compiler_params=_params(("parallel", "parallel", "arbitrary")),
        name="sb_attention",
    )(qkv, qkv, qkv)


def _merge_kernel(ya_ref, yb_ref, ga_ref, gb_ref, x_ref, wa_ref, wb_ref, wo_ref, g_ref,
                  x2_ref, h_ref):
    ya = jnp.dot(ya_ref[...], wa_ref[...], preferred_element_type=F32)
    yb = jnp.dot(yb_ref[...], wb_ref[...], preferred_element_type=F32)
    merged = ga_ref[...].astype(F32) * ya + gb_ref[...].astype(F32) * yb
    x2 = x_ref[...] + jnp.dot(merged.astype(BF16), wo_ref[...], preferred_element_type=F32)
    x2_ref[...] = x2
    h_ref[...] = _rms_scale(x2, g_ref[...]).astype(h_ref.dtype)


def _merge_project(ya, yb, gates, x2d, wa, wb, wo, g, tm):
    t, d = x2d.shape
    wa_w = ya.shape[1]
    wb_w = yb.shape[1]
    resident = lambda shape: pl.BlockSpec(shape, lambda i: (0, 0), pipeline_mode=pl.Buffered(1))
    return pl.pallas_call(
        _merge_kernel,
        grid=(t // tm,),
        in_specs=[
            pl.BlockSpec((tm, wa_w), lambda i: (i, 0)),
            pl.BlockSpec((tm, wb_w), lambda i: (i, 0)),
            pl.BlockSpec((tm, d), lambda i: (i, 0)),
            pl.BlockSpec((tm, d), lambda i: (i, 1)),
            pl.BlockSpec((tm, d), lambda i: (i, 0)),
            resident((wa_w, d)),
            resident((wb_w, d)),
            resident((d, d)),
            resident((1, d)),
        ],
        out_specs=[
            pl.BlockSpec((tm, d), lambda i: (i, 0)),
            pl.BlockSpec((tm, d), lambda i: (i, 0)),
        ],
        out_shape=[
            jax.ShapeDtypeStruct((t, d), F32),
            jax.ShapeDtypeStruct((t, d), BF16),
        ],
        compiler_params=_params(("parallel",)),
        name="merge_project",
    )(ya, yb, gates, gates, x2d, wa, wb, wo, g)


def _mlp_kernel(h_ref, wu_ref, wd_ref, x2_ref, g_ref, o_ref, *, final_norm):
    j = pl.program_id(1)

    @pl.when(j == 0)
    def _():
        o_ref[...] = x2_ref[...]

    u = jnp.maximum(jnp.dot(h_ref[...], wu_ref[...], preferred_element_type=F32), 0.0)
    o_ref[...] += jnp.dot((u * u).astype(BF16), wd_ref[...], preferred_element_type=F32)

    if final_norm:
        @pl.when(j == pl.num_programs(1) - 1)
        def _():
            o_ref[...] = _rms_scale(o_ref[...], g_ref[...])


def _mlp(h, wu, wd, x2, g, tm, tf, final_norm):
    t, d = x2.shape
    f = wu.shape[1]
    return pl.pallas_call(
        functools.partial(_mlp_kernel, final_norm=final_norm),
        grid=(t // tm, f // tf),
        in_specs=[
            pl.BlockSpec((tm, d), lambda i, j: (i, 0)),
            pl.BlockSpec((d, tf), lambda i, j: (0, j)),
            pl.BlockSpec((tf, d), lambda i, j: (j, 0)),
            pl.BlockSpec((tm, d), lambda i, j: (i, 0)),
            pl.BlockSpec((1, d), lambda i, j: (0, 0)),
        ],
        out_specs=pl.BlockSpec((tm, d), lambda i, j: (i, 0)),
        out_shape=jax.ShapeDtypeStruct((t, d), F32),
        compiler_params=_params(("parallel", "arbitrary")),
        name="mlp",
    )(h, wu, wd, x2, g)


def _tile(n, want):
    t = min(n, want)
    while n % t:
        t //= 2
    return t


def kernel(x, norm_mix_g, w_in, b_forget, w_out_fox, w_out_sb, w_out, norm_mlp_g, w_mlp_up,
           w_mlp_down, norm_final_g):
    batch, seq, d = x.shape
    depth = w_in.shape[0]
    n_heads_fox = b_forget.shape[-1]
    width_fox = w_out_fox.shape[1]
    width_sb = w_out_sb.shape[1]
    n_heads_sb = width_sb // HEAD_DIM
    assert width_fox == n_heads_fox * HEAD_DIM and n_heads_fox == n_heads_sb
    assert n_heads_fox <= 8 and seq % 4096 == 0 and d % V7X_LANES == 0
    t = batch * seq
    x2d = x.reshape(t, d)

    for l in range(depth):
        w = w_in[l].astype(BF16)
        o_f = 3 * width_fox
        o_sb = o_f + n_heads_fox
        o_g = o_sb + 3 * width_sb
        w_qkv_fox = w[:, :o_f]
        w_qkv_sb = w[:, o_sb:o_g]
        w_f = jnp.pad(w[:, o_f:o_sb], ((0, 0), (0, V7X_LANES - n_heads_fox)))
        b_f = jnp.pad(b_forget[l], (0, V7X_LANES - n_heads_fox)).reshape(1, V7X_LANES)
        w_g = w[:, o_g:]

        xn, lf = _norm_forget(x2d, norm_mix_g[l].reshape(1, d), w_f, b_f, _tile(t, 1024))
        c = _forget_cumsum(lf.reshape(batch, seq, V7X_LANES), n_heads_fox, _tile(seq, 512))
        qkv = _proj_heads(xn, w_qkv_fox, w_qkv_sb, batch, n_heads_fox, _tile(seq, 2048))
        gates = _proj_gates(xn, w_g, _tile(t, 2048), _tile(2 * d, 1024))

        ya = _fox_attention(qkv, c, 512, 8).reshape(t, width_fox)
        yb = _sb_attention(qkv, 4096, 256).reshape(t, width_sb)

        x2d, h = _merge_project(ya, yb, gates, x2d, w_out_fox[l].astype(BF16),
                                w_out_sb[l].astype(BF16), w_out[l].astype(BF16),
                                norm_mlp_g[l].reshape(1, d), _tile(t, 512))
        x2d = _mlp(h, w_mlp_up[l].astype(BF16), w_mlp_down[l].astype(BF16), x2d,
                   norm_final_g.reshape(1, d), _tile(t, 512), _tile(w_mlp_up.shape[2], 2048),
                   final_norm=(l == depth - 1))
    return x2d.reshape(batch, seq, d)
```

```python
import functools

import jax
import jax.numpy as jnp
from jax import lax
from jax.experimental import pallas as pl
from jax.experimental.pallas import tpu as pltpu

HEAD_DIM = 128
RMS_EPS = 1e-6
MASKED_LOGIT = -1e30
EXP_IS_ZERO_BELOW = -104.0
LOG2_E = 1.4426950408889634
V7X_LANES = 128
V7X_VMEM_LIMIT_BYTES = 60 * 1024 * 1024

F32 = jnp.float32
BF16 = jnp.bfloat16


def _params(semantics, vmem_bytes=V7X_VMEM_LIMIT_BYTES):
    return pltpu.CompilerParams(dimension_semantics=semantics, vmem_limit_bytes=vmem_bytes)


def _log_sigmoid(u):
    return jnp.minimum(u, 0.0) - jnp.log1p(jnp.exp(-jnp.abs(u)))


def _rms_scale(x, g):
    ms = jnp.mean(x * x, axis=-1, keepdims=True)
    return x * lax.rsqrt(ms + RMS_EPS) * g


def _norm_forget_kernel(x_ref, g_ref, wf_ref, bf_ref, xn_ref, c_ref, carry_ref, *,
                        n_heads, tiles_per_seq, tc):
    @pl.when(pl.program_id(0) % tiles_per_seq == 0)
    def _():
        carry_ref[...] = jnp.zeros_like(carry_ref)

    xn = _rms_scale(x_ref[...], g_ref[...]).astype(BF16)
    xn_ref[...] = xn
    f = jnp.dot(xn, wf_ref[...], preferred_element_type=F32) + bf_ref[...]
    lf = _log_sigmoid(f)

    row = lax.broadcasted_iota(jnp.int32, (tc, tc), 0)
    col = lax.broadcasted_iota(jnp.int32, (tc, tc), 1)
    upper = (row <= col).astype(BF16)
    carry = carry_ref[:, :1]
    for k in range(x_ref.shape[0] // tc):
        rest = lf[k * tc:(k + 1) * tc].T[:n_heads, :]
        cs = carry
        for _ in range(3):
            piece = rest.astype(BF16)
            cs = cs + jnp.dot(piece, upper, preferred_element_type=F32)
            rest = rest - piece.astype(F32)
        c_ref[0, :, k * tc:(k + 1) * tc] = cs
        carry = cs[:, tc - 1:tc]
    carry_ref[...] = jnp.broadcast_to(carry, carry_ref.shape)


def _norm_forget(x2d, g, wf, bf, batch, n_heads, tm, tc):
    t, d = x2d.shape
    s = t // batch
    tiles_per_seq = s // tm
    return pl.pallas_call(
        functools.partial(_norm_forget_kernel, n_heads=n_heads, tiles_per_seq=tiles_per_seq,
                          tc=tc),
        grid=(t // tm,),
        in_specs=[
            pl.BlockSpec((tm, d), lambda i: (i, 0)),
            pl.BlockSpec((1, d), lambda i: (0, 0)),
            pl.BlockSpec((d, V7X_LANES), lambda i: (0, 0)),
            pl.BlockSpec((1, V7X_LANES), lambda i: (0, 0)),
        ],
        out_specs=[
            pl.BlockSpec((tm, d), lambda i: (i, 0)),
            pl.BlockSpec((1, n_heads, tm),
                         lambda i: (i // tiles_per_seq, 0, i % tiles_per_seq)),
        ],
        out_shape=[
            jax.ShapeDtypeStruct((t, d), BF16),
            jax.ShapeDtypeStruct((batch, n_heads, s), F32),
        ],
        scratch_shapes=[pltpu.VMEM((n_heads, V7X_LANES), F32)],
        compiler_params=_params(("arbitrary",)),
        name="norm_forget",
    )(x2d, g, wf, bf)


def _proj_heads_kernel(x_ref, wa_ref, wb_ref, o_ref, *, groups_a):
    j = pl.program_id(1)

    def emit(w_ref):
        acc = jnp.dot(x_ref[...], w_ref[...], preferred_element_type=F32)
        for hh in range(o_ref.shape[2]):
            o_ref[0, 0, hh] = acc[:, hh * HEAD_DIM:(hh + 1) * HEAD_DIM].astype(o_ref.dtype)

    @pl.when(j < groups_a)
    def _():
        emit(wa_ref)

    @pl.when(j >= groups_a)
    def _():
        emit(wb_ref)


def _proj_heads(xn, wa, wb, batch, n_heads, tm):
    t, d = xn.shape
    tn = n_heads * HEAD_DIM
    ga, gb = wa.shape[1] // tn, wb.shape[1] // tn
    s = t // batch
    nst = s // tm
    return pl.pallas_call(
        functools.partial(_proj_heads_kernel, groups_a=ga),
        grid=(t // tm, ga + gb),
        in_specs=[
            pl.BlockSpec((tm, d), lambda i, j: (i, 0)),
            pl.BlockSpec((d, tn), lambda i, j: (0, jnp.minimum(j, ga - 1))),
            pl.BlockSpec((d, tn), lambda i, j: (0, jnp.maximum(j - ga, 0))),
        ],
        out_specs=pl.BlockSpec((1, 1, n_heads, tm, HEAD_DIM),
                               lambda i, j: (j, i // nst, 0, i % nst, 0)),
        out_shape=jax.ShapeDtypeStruct((ga + gb, batch, n_heads, s, HEAD_DIM), BF16),
        compiler_params=_params(("parallel", "arbitrary")),
        name="proj_heads",
    )(xn, wa, wb)


def _proj_gate_kernel(x_ref, w_ref, o_ref):
    acc = jnp.dot(x_ref[...], w_ref[...], preferred_element_type=F32)
    o_ref[...] = (0.5 * jnp.tanh(0.5 * acc) + 0.5).astype(o_ref.dtype)


def _proj_gates(xn, w, tm, tn):
    t, d = xn.shape
    n = w.shape[1]
    return pl.pallas_call(
        _proj_gate_kernel,
        grid=(t // tm, n // tn),
        in_specs=[
            pl.BlockSpec((tm, d), lambda i, j: (i, 0)),
            pl.BlockSpec((d, tn), lambda i, j: (0, j)),
        ],
        out_specs=pl.BlockSpec((tm, tn), lambda i, j: (i, j)),
        out_shape=jax.ShapeDtypeStruct((t, n), BF16),
        compiler_params=_params(("parallel", "arbitrary")),
        name="proj_gates",
    )(xn, w)


def _fox_kernel(cstart_ref, cend_ref, q_ref, k_ref, v_ref, c_ref, o_ref, kn2_ref, *,
                n_heads, blk, scale):
    b, h, step = pl.program_id(0), pl.program_id(1), pl.program_id(2)
    bh = b * n_heads + h
    tiles = q_ref.shape[3] // blk
    key = lax.broadcasted_iota(jnp.int32, (blk, blk), 0)
    query = lax.broadcasted_iota(jnp.int32, (blk, blk), 1)
    ones_rows = jnp.ones((8, HEAD_DIM), BF16)

    @pl.when(step == 0)
    def _():
        kn2_ref[0] = jnp.float32(0.0)

    kf = k_ref[0, 0, 0, pl.ds(step * (tiles * blk), tiles * blk), :].astype(F32)
    kn2_ref[0] = jnp.maximum(kn2_ref[0], jnp.max(jnp.sum(kf * kf, axis=-1)))
    kn2 = kn2_ref[0]

    def logits_t(q2, j, shift):
        kblk = k_ref[0, 0, 0, pl.ds(j * blk, blk), :]
        s = lax.dot_general(kblk, q2, (((1,), (1,)), ((), ())), preferred_element_type=F32)
        bias_row = (shift - c_ref[0, j]) * LOG2_E
        bias_col = jnp.broadcast_to(bias_row, (V7X_LANES, blk)).T
        return s + jnp.concatenate([bias_col] * (blk // V7X_LANES), axis=1)

    def weighted_values_t(p_t, j):
        vblk = v_ref[0, 0, 0, pl.ds(j * blk, blk), :]
        return lax.dot_general(vblk, p_t.astype(BF16), (((0,), (0,)), ((), ())),
                               preferred_element_type=F32)

    def near_pass(t):
        qi = step * tiles + t
        c0 = cstart_ref[bh, qi]
        q2 = (q_ref[0, 0, 0, t * blk:(t + 1) * blk, :].astype(F32)
              * (scale * LOG2_E)).astype(BF16)
        j_prev = jnp.maximum(qi - 1, 0)
        s_prev = logits_t(q2, j_prev, c0 + jnp.where(qi >= 1, 0.0, MASKED_LOGIT))
        s_diag = jnp.where(key <= query, logits_t(q2, qi, c0), MASKED_LOGIT)
        m = jnp.maximum(jnp.max(s_prev, axis=0, keepdims=True),
                        jnp.max(s_diag, axis=0, keepdims=True))
        p_prev = jnp.exp2(s_prev - m)
        p_diag = jnp.exp2(s_diag - m)
        l = jnp.sum(p_prev, axis=0, keepdims=True) + jnp.sum(p_diag, axis=0, keepdims=True)
        acc = weighted_values_t(p_prev, j_prev) + weighted_values_t(p_diag, qi)
        q2f = q2.astype(F32)
        qn2 = lax.dot_general(ones_rows, (q2f * q2f).astype(BF16), (((1,), (1,)), ((), ())),
                              preferred_element_type=F32)[:1]
        excess = jnp.max(jnp.sqrt(qn2 * (1.01 * kn2)) - m)
        return qi, c0, q2, m, l, acc, excess

    def far_pass(qi, c0, q2, m, l, acc, excess):
        def forget_gap(j):
            return (c0 - cend_ref[bh, jnp.maximum(j, 0)]) * LOG2_E

        def count_cond(n):
            j = qi - 2 - n
            return jnp.logical_and(j >= 0,
                                   excess + forget_gap(j) >= EXP_IS_ZERO_BELOW * LOG2_E)

        n_blocks = lax.while_loop(count_cond, lambda n: n + 1, jnp.int32(0))

        def general_body(n, carry):
            m, l, acc = carry
            j = qi - 2 - n
            s = logits_t(q2, j, c0)
            m_new = jnp.maximum(m, jnp.max(s, axis=0, keepdims=True))
            p = jnp.exp2(s - m_new)
            alpha = jnp.exp2(m - m_new)
            l = alpha * l + jnp.sum(p, axis=0, keepdims=True)
            acc = alpha * acc + weighted_values_t(p, j)
            return m_new, l, acc

        def general_loop(_):
            return lax.fori_loop(0, n_blocks, general_body, (m, l, acc))[1:]

        def fixed_max_body(n, carry):
            l, acc = carry
            ja = qi - 2 - 2 * n
            p_a = jnp.exp2(logits_t(q2, ja, c0) - m)
            p_b = jnp.exp2(logits_t(q2, ja - 1, c0) - m)
            l = l + jnp.sum(p_a, axis=0, keepdims=True) + jnp.sum(p_b, axis=0, keepdims=True)
            acc = acc + weighted_values_t(p_a, ja) + weighted_values_t(p_b, ja - 1)
            return l, acc

        def fixed_max_loop(_):
            l2, acc2 = lax.fori_loop(0, lax.shift_right_logical(n_blocks, jnp.int32(1)), fixed_max_body,
                                     (l, acc))

            def last_block(_):
                j = qi - 1 - n_blocks
                p = jnp.exp2(logits_t(q2, j, c0) - m)
                return l2 + jnp.sum(p, axis=0, keepdims=True), acc2 + weighted_values_t(p, j)

            return lax.cond(jnp.bitwise_and(n_blocks, 1) == 1, last_block,
                            lambda _: (l2, acc2), None)

        max_is_final = excess + forget_gap(qi - 2) <= 0.0
        l, acc = lax.cond(max_is_final, fixed_max_loop, general_loop, None)
        return (acc / l).T

    near = [near_pass(t) for t in range(tiles)]
    for t, state in enumerate(near):
        o_ref[0, t * blk:(t + 1) * blk, :] = far_pass(*state).astype(o_ref.dtype)


def _fox_attention(qkv, c, blk, tiles):
    _, batch, n_heads, seq, _ = qkv.shape
    nb = seq // blk
    rows = tiles * blk
    cflat = c.reshape(batch * n_heads, nb, blk)
    cstart = cflat[:, :, 0]
    cend = cflat[:, :, blk - 1]
    cblocks = cflat.reshape(batch * n_heads, nb, 1, blk)
    kernel = functools.partial(_fox_kernel, n_heads=n_heads, blk=blk, scale=HEAD_DIM ** -0.5)
    grid_spec = pltpu.PrefetchScalarGridSpec(
        num_scalar_prefetch=2,
        grid=(batch, n_heads, seq // rows),
        in_specs=[
            pl.BlockSpec((1, 1, 1, rows, HEAD_DIM), lambda b, h, i, *_: (0, b, h, i, 0)),
            pl.BlockSpec((1, 1, 1, seq, HEAD_DIM), lambda b, h, i, *_: (1, b, h, 0, 0)),
            pl.BlockSpec((1, 1, 1, seq, HEAD_DIM), lambda b, h, i, *_: (2, b, h, 0, 0)),
            pl.BlockSpec((1, nb, 1, blk), lambda b, h, i, *_: (b * n_heads + h, 0, 0, 0)),
        ],
        out_specs=pl.BlockSpec((1, rows, HEAD_DIM), lambda b, h, i, *_: (b, i, h)),
        scratch_shapes=[pltpu.SMEM((1,), F32)],
    )
    return pl.pallas_call(
        kernel,
        grid_spec=grid_spec,
        out_shape=jax.ShapeDtypeStruct((batch, seq, n_heads * HEAD_DIM), BF16),
        compiler_params=_params(("parallel", "parallel", "arbitrary")),
        name="fox_attention",
    )(cstart, cend, qkv, qkv, qkv, cblocks)


def _sb_kernel(q_ref, k_ref, v_ref, o_ref, *, sub, scale):
    qi = pl.program_id(2)
    n_sub = q_ref.shape[3] // sub
    row = lax.broadcasted_iota(jnp.int32, (sub, sub), 0)
    col = lax.broadcasted_iota(jnp.int32, (sub, sub), 1)
    later = (row > col).astype(BF16)
    strict = col < row

    def block(y_q, j, carry, acc, masked, y_bias=None):
        kblk = k_ref[0, 0, 0, pl.ds(j * sub, sub), :]
        vblk = v_ref[0, 0, 0, pl.ds(j * sub, sub), :]
        y = lax.dot_general(y_q, kblk, (((1,), (1,)), ((), ())), preferred_element_type=F32)
        if y_bias is not None:
            y = y + y_bias
        if masked:
            y = jnp.where(strict, y, -MASKED_LOGIT)
        log2_not_beta = jnp.minimum(y, 0.0) - jnp.log2(1.0 + jnp.exp2(-jnp.abs(y)))
        after = jnp.dot(log2_not_beta.astype(BF16), later, preferred_element_type=F32) + carry
        a = jnp.exp2(log2_not_beta - y + after)
        acc = acc + jnp.dot(a.astype(BF16), vblk, preferred_element_type=F32)
        carry = carry + jnp.sum(log2_not_beta, axis=-1, keepdims=True)
        return carry, acc

    def log2_sigmoid(y):
        return jnp.minimum(y, 0.0) - jnp.log2(1.0 + jnp.exp2(-jnp.abs(y)))

    def near_window(y_q, jd):
        kwin = k_ref[0, 0, 0, pl.ds((jd - 1) * sub, 2 * sub), :]
        vwin = v_ref[0, 0, 0, pl.ds((jd - 1) * sub, 2 * sub), :]
        y = lax.dot_general(y_q, kwin, (((1,), (1,)), ((), ())), preferred_element_type=F32)
        y_prev = y[:, :sub]
        y_diag = jnp.where(strict, y[:, sub:], -MASKED_LOGIT)
        lnb_prev = log2_sigmoid(y_prev)
        lnb_diag = log2_sigmoid(y_diag)
        after_diag = jnp.dot(lnb_diag.astype(BF16), later, preferred_element_type=F32)
        carry_diag = jnp.sum(lnb_diag, axis=-1, keepdims=True)
        after_prev = (jnp.dot(lnb_prev.astype(BF16), later, preferred_element_type=F32)
                      + carry_diag)
        a = jnp.concatenate([jnp.exp2(lnb_prev - y_prev + after_prev),
                             jnp.exp2(lnb_diag - y_diag + after_diag)], axis=1)
        acc = jnp.dot(a.astype(BF16), vwin, preferred_element_type=F32)
        return carry_diag + jnp.sum(lnb_prev, axis=-1, keepdims=True), acc

    states = []
    for t in range(n_sub):
        y_q = (q_ref[0, 0, 0, t * sub:(t + 1) * sub, :].astype(F32)
               * (-scale * LOG2_E)).astype(BF16)
        jd = qi * n_sub + t
        if t == 0:
            carry, acc = block(y_q, jd, jnp.zeros((sub, 1), F32),
                               jnp.zeros((sub, HEAD_DIM), F32), True)
            carry, acc = block(y_q, jnp.maximum(jd - 1, 0), carry, acc, False,
                               y_bias=jnp.where(jd >= 1, 0.0, -MASKED_LOGIT))
        else:
            carry, acc = near_window(y_q, jd)
        states.append((y_q, jd, carry, acc))

    def walk_back(_):
        accs = []
        for y_q, jd, carry, acc in states:
            def cond(state):
                j, carry, _ = state
                return jnp.logical_and(j >= 0, jnp.max(carry) >= EXP_IS_ZERO_BELOW * LOG2_E)

            def body(state, y_q=y_q):
                j, carry, acc = state
                carry, acc = block(y_q, j, carry, acc, False)
                return j - 1, carry, acc

            accs.append(lax.while_loop(cond, body, (jd - 2, carry, acc))[2])
        return accs

    highest_carry = jnp.max(functools.reduce(jnp.maximum, [s[2] for s in states]))
    accs = lax.cond(highest_carry >= EXP_IS_ZERO_BELOW * LOG2_E, walk_back,
                    lambda _: [s[3] for s in states], None)
    for t, acc in enumerate(accs):
        o_ref[0, t * sub:(t + 1) * sub, :] = acc.astype(o_ref.dtype)


def _sb_attention(qkv, blk, sub):
    _, batch, n_heads, seq, _ = qkv.shape
    nb = seq // blk
    kernel = functools.partial(_sb_kernel, sub=sub, scale=HEAD_DIM ** -0.5)
    return pl.pallas_call(
        kernel,
        grid=(batch, n_heads, nb),
        in_specs=[
            pl.BlockSpec((1, 1, 1, blk, HEAD_DIM), lambda b, h, i: (3, b, h, i, 0)),
            pl.BlockSpec((1, 1, 1, seq, HEAD_DIM), lambda b, h, i: (4, b, h, 0, 0)),
            pl.BlockSpec((1, 1, 1, seq, HEAD_DIM), lambda b, h, i: (5, b, h, 0, 0)),
        ],
        out_specs=pl.BlockSpec((1, blk, HEAD_DIM), lambda b, h, i: (b, i, h)),
        out_shape=jax.ShapeDtypeStruct((batch, seq, n_heads * HEAD_DIM), BF16),
        compiler_params=_params(("parallel", "parallel", "arbitrary")),
        name="sb_attention",
    )(qkv, qkv, qkv)


def _merge_kernel(ya_ref, yb_ref, ga_ref, gb_ref, x_ref, wa_ref, wb_ref, wo_ref, g_ref,
                  x2_ref, h_ref):
    ya = jnp.dot(ya_ref[...], wa_ref[...], preferred_element_type=F32)
    yb = jnp.dot(yb_ref[...], wb_ref[...], preferred_element_type=F32)
    merged = ga_ref[...].astype(F32) * ya + gb_ref[...].astype(F32) * yb
    x2 = x_ref[...] + jnp.dot(merged.astype(BF16), wo_ref[...], preferred_element_type=F32)
    x2_ref[...] = x2
    h_ref[...] = _rms_scale(x2, g_ref[...]).astype(h_ref.dtype)


def _merge_project(ya, yb, gates, x2d, wa, wb, wo, g, tm):
    t, d = x2d.shape
    wa_w = ya.shape[1]
    wb_w = yb.shape[1]
    resident = lambda shape: pl.BlockSpec(shape, lambda i: (0, 0), pipeline_mode=pl.Buffered(1))
    return pl.pallas_call(
        _merge_kernel,
        grid=(t // tm,),
        in_specs=[
            pl.BlockSpec((tm, wa_w), lambda i: (i, 0)),
            pl.BlockSpec((tm, wb_w), lambda i: (i, 0)),
            pl.BlockSpec((tm, d), lambda i: (i, 0)),
            pl.BlockSpec((tm, d), lambda i: (i, 1)),
            pl.BlockSpec((tm, d), lambda i: (i, 0)),
            resident((wa_w, d)),
            resident((wb_w, d)),
            resident((d, d)),
            resident((1, d)),
        ],
        out_specs=[
            pl.BlockSpec((tm, d), lambda i: (i, 0)),
            pl.BlockSpec((tm, d), lambda i: (i, 0)),
        ],
        out_shape=[
            jax.ShapeDtypeStruct((t, d), F32),
            jax.ShapeDtypeStruct((t, d), BF16),
        ],
        compiler_params=_params(("parallel",)),
        name="merge_project",
    )(ya, yb, gates, gates, x2d, wa, wb, wo, g)


def _mlp_kernel(h_ref, wu_ref, wd_ref, x2_ref, g_ref, o_ref, *, final_norm):
    j = pl.program_id(1)

    @pl.when(j == 0)
    def _():
        o_ref[...] = x2_ref[...]

    u = jnp.maximum(jnp.dot(h_ref[...], wu_ref[...], preferred_element_type=F32), 0.0)
    o_ref[...] += jnp.dot((u * u).astype(BF16), wd_ref[...], preferred_element_type=F32)

    if final_norm:
        @pl.when(j == pl.num_programs(1) - 1)
        def _():
            o_ref[...] = _rms_scale(o_ref[...], g_ref[...])


def _mlp(h, wu, wd, x2, g, tm, tf, final_norm):
    t, d = x2.shape
    f = wu.shape[1]
    return pl.pallas_call(
        functools.partial(_mlp_kernel, final_norm=final_norm),
        grid=(t // tm, f // tf),
        in_specs=[
            pl.BlockSpec((tm, d), lambda i, j: (i, 0)),
            pl.BlockSpec((d, tf), lambda i, j: (0, j)),
            pl.BlockSpec((tf, d), lambda i, j: (j, 0)),
            pl.BlockSpec((tm, d), lambda i, j: (i, 0)),
            pl.BlockSpec((1, d), lambda i, j: (0, 0)),
        ],
        out_specs=pl.BlockSpec((tm, d), lambda i, j: (i, 0)),
        out_shape=jax.ShapeDtypeStruct((t, d), F32),
        compiler_params=_params(("parallel", "arbitrary")),
        name="mlp",
    )(h, wu, wd, x2, g)


def _tile(n, want):
    t = min(n, want)
    while n % t:
        t //= 2
    return t


def kernel(x, norm_mix_g, w_in, b_forget, w_out_fox, w_out_sb, w_out, norm_mlp_g, w_mlp_up,
           w_mlp_down, norm_final_g):
    batch, seq, d = x.shape
    depth = w_in.shape[0]
    n_heads_fox = b_forget.shape[-1]
    width_fox = w_out_fox.shape[1]
    width_sb = w_out_sb.shape[1]
    n_heads_sb = width_sb // HEAD_DIM
    assert width_fox == n_heads_fox * HEAD_DIM and n_heads_fox == n_heads_sb
    assert n_heads_fox <= 8 and seq % 4096 == 0 and d % V7X_LANES == 0
    t = batch * seq
    x2d = x.reshape(t, d)

    for l in range(depth):
        w = w_in[l].astype(BF16)
        o_f = 3 * width_fox
        o_sb = o_f + n_heads_fox
        o_g = o_sb + 3 * width_sb
        w_qkv_fox = w[:, :o_f]
        w_qkv_sb = w[:, o_sb:o_g]
        w_f = jnp.pad(w[:, o_f:o_sb], ((0, 0), (0, V7X_LANES - n_heads_fox)))
        b_f = jnp.pad(b_forget[l], (0, V7X_LANES - n_heads_fox)).reshape(1, V7X_LANES)
        w_g = w[:, o_g:]

        xn, c = _norm_forget(x2d, norm_mix_g[l].reshape(1, d), w_f, b_f, batch, n_heads_fox,
                             _tile(seq, 1024), _tile(seq, 512))
        qkv = _proj_heads(xn, w_qkv_fox, w_qkv_sb, batch, n_heads_fox, _tile(seq, 2048))
        gates = _proj_gates(xn, w_g, _tile(t, 2048), _tile(2 * d, 1024))

        ya = _fox_attention(qkv, c, 512, 8).reshape(t, width_fox)
        yb = _sb_attention(qkv, 4096, 256).reshape(t, width_sb)

        x2d, h = _merge_project(ya, yb, gates, x2d, w_out_fox[l].astype(BF16),
                                w_out_sb[l].astype(BF16), w_out[l].astype(BF16),
                                norm_mlp_g[l].reshape(1, d), _tile(t, 512))
        x2d = _mlp(h, w_mlp_up[l].astype(BF16), w_mlp_down[l].astype(BF16), x2d,
                   norm_final_g.reshape(1, d), _tile(t, 512), _tile(w_mlp_up.shape[2], 2048),
                   final_norm=(l == depth - 1))
    return x2d.reshape(batch, seq, d)
```
